```python
import jax, jax.numpy as jnp
from jax import lax
import numpy as np

D_MODEL = 2048
BATCH = 4
SEQ = 4096
DEPTH = 2

GRID_W = 64
CTX_LEN = 256
N_BRANCH = 4
BRANCH_W = D_MODEL // 4
RET_HEADS = 4
RET_DK = BRANCH_W // RET_HEADS
RET_CHUNK = 128
ROPE_PAIRS = RET_DK // 4
ROPE_BASE = 10000.0
FNET_GROUPS = 4
FNET_GW = BRANCH_W // FNET_GROUPS
SC_WIDTH = 3
CF_WIDTH = 31
D_FF = 4 * D_MODEL
IN_COLS = 10 * BRANCH_W + N_BRANCH * D_MODEL
EPS = 1e-6

kernel_name = "hybrid_retention_fnet_conv_dit_block"


def rms_norm(x, g):
    x32 = x.astype(jnp.float32)
    y = x32 * lax.rsqrt(jnp.mean(x32 * x32, axis=-1, keepdims=True) + EPS)
    return (y * g.astype(jnp.float32)).astype(x.dtype)


def layer_norm(x, g, b):
    x32 = x.astype(jnp.float32)
    mu = jnp.mean(x32, axis=-1, keepdims=True)
    var = jnp.mean(jnp.square(x32 - mu), axis=-1, keepdims=True)
    y = (x32 - mu) * lax.rsqrt(var + EPS)
    return (y * g.astype(jnp.float32) + b.astype(jnp.float32)).astype(x.dtype)


def depthwise_conv(u, w):
    return lax.conv_general_dilated(u, w[:, None, :].astype(u.dtype), window_strides=(1,), padding="SAME",
                                    dimension_numbers=("NWC", "WIO", "NWC"), feature_group_count=u.shape[-1])


def fourier_mix(u):
    b, l, _ = u.shape
    ug = u.astype(jnp.float32).reshape(b, l, FNET_GROUPS, FNET_GW)
    y = jnp.real(jnp.fft.fftn(ug, axes=(1, 3), norm="ortho"))
    return y.reshape(b, l, BRANCH_W).astype(u.dtype)


def _heads(t):
    b, l, _ = t.shape
    return t.astype(jnp.float32).reshape(b, l, RET_HEADS, RET_DK)


def rope2d(t, cos, sin):
    half = RET_DK // 2
    t1, t2 = t[..., :half], t[..., half:]
    cs, sn = cos[None, :, None, :], sin[None, :, None, :]
    return jnp.concatenate([t1 * cs - t2 * sn, t1 * sn + t2 * cs], axis=-1)


def retention_chunkwise(q, k, v, log_g, s0, strict):
    b, l, h, d = q.shape
    n = l // RET_CHUNK
    idx = jnp.arange(RET_CHUNK, dtype=jnp.float32)
    diff = idx[:, None] - idx[None, :]
    keep = (diff > 0) if strict else (diff >= 0)
    intra = jnp.where(keep[None], jnp.exp(log_g[:, None, None] * jnp.maximum(diff, 0.0)[None]), 0.0)
    q_dec = jnp.exp(log_g[:, None] * (idx + 1.0)[None])
    k_dec = jnp.exp(log_g[:, None] * (RET_CHUNK - 1.0 - idx)[None])
    chunk_dec = jnp.exp(log_g * RET_CHUNK)

    def to_chunks(t):
        return t.reshape(b, n, RET_CHUNK, h, d).transpose(1, 0, 3, 2, 4)

    def step(s, qkv):
        qc, kc, vc = qkv
        scores = jnp.einsum("bhqd,bhkd->bhqk", qc, kc) * intra
        y = jnp.einsum("bhqk,bhkv->bhqv", scores, vc) + jnp.einsum("bhqd,bhdv->bhqv", qc * q_dec[..., None], s)
        s = chunk_dec[:, None, None] * s + jnp.einsum("bhkd,bhkv->bhdv", kc * k_dec[..., None], vc)
        return s, y

    _, y = lax.scan(step, s0, (to_chunks(q), to_chunks(k), to_chunks(v)))
    return y.transpose(1, 0, 3, 2, 4).reshape(b, l, h, d)


def context_state(k, v, log_g):
    l = k.shape[1]
    w = jnp.exp(log_g[:, None] * (l - 1.0 - jnp.arange(l, dtype=jnp.float32))[None])
    return jnp.einsum("blhd,blhv,hl->bhdv", k, v, w)


def bidir_retention(q, k, v, lg_f, lg_b, s_f, s_b):
    y_f = retention_chunkwise(q, k, v, lg_f, s_f, False)
    y_b = retention_chunkwise(q[:, ::-1], k[:, ::-1], v[:, ::-1], lg_b, s_b, True)[:, ::-1]
    return y_f + y_b


def retention_out(y, gate):
    mu = jnp.mean(y, axis=-1, keepdims=True)
    var = jnp.mean(jnp.square(y - mu), axis=-1, keepdims=True)
    yn = ((y - mu) * lax.rsqrt(var + EPS)).reshape(y.shape[0], y.shape[1], BRANCH_W)
    return yn.astype(gate.dtype) * jax.nn.silu(gate)


def merge_branches(parts, y_ret, sc_conv, cf_conv, cf_ln, w_branch, w_out):
    u_f, sc_b, sc_c, sc_x, cf_a, cf_b, gates = parts[4:]
    y_f = fourier_mix(u_f)
    y_sc = sc_b * depthwise_conv(sc_c * sc_x, sc_conv)
    y_cf = jax.nn.silu(layer_norm(depthwise_conv(cf_a * jax.nn.sigmoid(cf_b), cf_conv), cf_ln[0], cf_ln[1]))
    ybr = jnp.stack([y_ret, y_f, y_sc, y_cf], axis=2)
    proj = jnp.einsum("blnw,nwd->blnd", ybr, w_branch)
    b, l, _ = gates.shape
    g = jax.nn.sigmoid(gates.reshape(b, l, N_BRANCH, D_MODEL))
    return jnp.sum(g * proj, axis=2) @ w_out


def sq_relu_mlp(h, w1, w2):
    return jnp.square(jax.nn.relu(h @ w1)) @ w2


def hybrid_layer(x, xc, mod, mod_c, w_in, norm_g, ret_decay, sc_conv, cf_conv, cf_ln, w_branch, w_out,
                 w_ff1, w_ff2, cos, sin, with_ctx):
    sh1, sc1, gt1, sh2, sc2, gt2 = jnp.split(mod[:, None, :], 6, axis=-1)
    csh1, csc1, cgt1, csh2, csc2, cgt2 = jnp.split(mod_c, 6, axis=-1)
    splits = [BRANCH_W * i for i in range(1, 11)]

    h = rms_norm(x, norm_g[0]) * (1.0 + sc1) + sh1
    hc = rms_norm(xc, norm_g[0]) * (1.0 + csc1) + csh1
    parts = jnp.split(h @ w_in, splits, axis=-1)
    if with_ctx:
        parts_c = jnp.split(hc @ w_in, splits, axis=-1)
    else:
        parts_c = jnp.split(hc @ w_in[:, :2 * BRANCH_W], [BRANCH_W], axis=-1)

    lg_f, lg_b = jax.nn.log_sigmoid(ret_decay.astype(jnp.float32))
    k_scale = RET_DK ** -0.5
    k_c = _heads(parts_c[0]) * k_scale
    v_c = _heads(parts_c[1])
    s_f = context_state(k_c, v_c, lg_f)
    s_b = context_state(k_c[:, ::-1], v_c[:, ::-1], lg_b)

    q = rope2d(_heads(parts[2]), cos, sin)
    k = rope2d(_heads(parts[0]), cos, sin) * k_scale
    v = _heads(parts[1])
    y_ret = retention_out(bidir_retention(q, k, v, lg_f, lg_b, s_f, s_b), parts[3])
    mix = merge_branches(parts, y_ret, sc_conv, cf_conv, cf_ln, w_branch, w_out)
    x = x + gt1 * rms_norm(mix, norm_g[1])
    h2 = rms_norm(x, norm_g[2]) * (1.0 + sc2) + sh2
    x = x + gt2 * rms_norm(sq_relu_mlp(h2, w_ff1, w_ff2), norm_g[3])

    if with_ctx:
        zeros = jnp.zeros_like(s_f)
        y_ret_c = retention_out(bidir_retention(_heads(parts_c[2]), k_c, v_c, lg_f, lg_b, zeros, zeros), parts_c[3])
        mix_c = merge_branches(parts_c, y_ret_c, sc_conv, cf_conv, cf_ln, w_branch, w_out)
        xc = xc + cgt1 * rms_norm(mix_c, norm_g[1])
        hc2 = rms_norm(xc, norm_g[2]) * (1.0 + csc2) + csh2
        xc = xc + cgt2 * rms_norm(sq_relu_mlp(hc2, w_ff1, w_ff2), norm_g[3])
    return x, xc


def setup_inputs(seed: int = 0) -> dict:
    key = jax.random.key(seed)
    ks = jax.random.split(key, 18)
    f32 = jnp.float32

    def nrm(k, shape, scale):
        return jax.random.normal(k, shape, f32) * scale

    x = nrm(ks[0], (BATCH, SEQ, D_MODEL), 1.0)
    c = nrm(ks[1], (BATCH, D_MODEL), 1.0)
    ctx = nrm(ks[2], (BATCH, CTX_LEN, D_MODEL), 1.0)
    c_ctx = nrm(ks[3], (D_MODEL,), 1.0)
    w_ada = nrm(ks[4], (DEPTH, D_MODEL, 6 * D_MODEL), 0.5 * D_MODEL ** -0.5)
    b_ada = nrm(ks[5], (DEPTH, 6 * D_MODEL), 0.02)
    norm_g = 1.0 + nrm(ks[6], (DEPTH, 4, D_MODEL), 0.02)
    w_in = nrm(ks[7], (DEPTH, D_MODEL, IN_COLS), D_MODEL ** -0.5)
    base = jnp.log(2.0 ** (5.0 + jnp.arange(RET_HEADS, dtype=f32)) - 1.0)
    ret_decay = base[None, None, :] + nrm(ks[8], (DEPTH, 2, RET_HEADS), 0.1)
    sc_conv = nrm(ks[9], (DEPTH, SC_WIDTH, BRANCH_W), SC_WIDTH ** -0.5)
    cf_conv = nrm(ks[10], (DEPTH, CF_WIDTH, BRANCH_W), CF_WIDTH ** -0.5)
    cf_ln = jnp.stack([1.0 + nrm(ks[11], (DEPTH, BRANCH_W), 0.02), nrm(ks[12], (DEPTH, BRANCH_W), 0.02)], axis=1)
    w_branch = nrm(ks[13], (DEPTH, N_BRANCH, BRANCH_W, D_MODEL), BRANCH_W ** -0.5)
    w_out = nrm(ks[14], (DEPTH, D_MODEL, D_MODEL), D_MODEL ** -0.5)
    w_ff1 = nrm(ks[15], (DEPTH, D_MODEL, D_FF), D_MODEL ** -0.5)
    w_ff2 = nrm(ks[16], (DEPTH, D_FF, D_MODEL), D_FF ** -0.5)
    return {"x": x, "c": c, "ctx": ctx, "c_ctx": c_ctx, "w_ada": w_ada, "b_ada": b_ada, "norm_g": norm_g,
            "w_in": w_in, "ret_decay": ret_decay, "sc_conv": sc_conv, "cf_conv": cf_conv, "cf_ln": cf_ln,
            "w_branch": w_branch, "w_out": w_out, "w_ff1": w_ff1, "w_ff2": w_ff2}


def reference(x, c, ctx, c_ctx, w_ada, b_ada, norm_g, w_in, ret_decay, sc_conv, cf_conv, cf_ln,
              w_branch, w_out, w_ff1, w_ff2):
    n_tok = x.shape[1]
    rows = n_tok // GRID_W
    row = jnp.repeat(jnp.arange(rows, dtype=jnp.float32), GRID_W)
    col = jnp.tile(jnp.arange(GRID_W, dtype=jnp.float32), rows)
    freqs = ROPE_BASE ** (-jnp.arange(ROPE_PAIRS, dtype=jnp.float32) / ROPE_PAIRS)
    ang = jnp.concatenate([row[:, None] * freqs[None], col[:, None] * freqs[None]], axis=-1)
    cos, sin = jnp.cos(ang), jnp.sin(ang)
    xc = ctx
    for layer in range(DEPTH):
        mod = jax.nn.silu(c) @ w_ada[layer] + b_ada[layer]
        mod_c = jax.nn.silu(c_ctx) @ w_ada[layer] + b_ada[layer]
        x, xc = hybrid_layer(x, xc, mod, mod_c, w_in[layer], norm_g[layer], ret_decay[layer], sc_conv[layer],
                             cf_conv[layer], cf_ln[layer], w_branch[layer], w_out[layer], w_ff1[layer],
                             w_ff2[layer], cos, sin, layer < DEPTH - 1)
    return x
```

```python
import functools
import math

import numpy as np
import jax
import jax.numpy as jnp
from jax import lax
from jax.experimental import pallas as pl
from jax.experimental.pallas import tpu as pltpu

F32 = jnp.float32
BF16 = jnp.bfloat16

D_MODEL = 2048
GRID_W = 64
N_BRANCH = 4
BRANCH_W = D_MODEL // 4
RET_HEADS = 4
RET_DK = BRANCH_W // RET_HEADS
RET_CHUNK = 128
ROPE_PAIRS = RET_DK // 4
ROPE_BASE = 10000.0
FNET_GROUPS = 4
FNET_GW = BRANCH_W // FNET_GROUPS
SC_WIDTH = 3
CF_WIDTH = 31
D_FF = 4 * D_MODEL
GATE_COL0 = 10 * BRANCH_W
IN_COLS = GATE_COL0 + N_BRANCH * D_MODEL
EPS = 1e-6
K_SCALE = RET_DK ** -0.5

VMEM_LIMIT_BYTES = 56 * 1024 * 1024
HALO_ROWS = 16


def _params(*sem):
    return pltpu.CompilerParams(dimension_semantics=sem, vmem_limit_bytes=VMEM_LIMIT_BYTES)


def _sigmoid(v):
    return 1.0 / (1.0 + jnp.exp(-v))


def _log_sigmoid(v):
    return jnp.minimum(v, 0.0) - jnp.log1p(jnp.exp(-jnp.abs(v)))


def _rms(v):
    return v * lax.rsqrt(jnp.mean(v * v, axis=-1, keepdims=True) + EPS)


def _ada_kernel(c_ref, w_ref, b_ref, o_ref):
    c = c_ref[...]
    s = c * _sigmoid(c)
    o_ref[...] = jnp.dot(s, w_ref[...], preferred_element_type=F32,
                         precision=lax.Precision.HIGHEST) + b_ref[...]


def _ada(c_rows, w_ada, b_ada):
    depth, d, n = w_ada.shape
    rows = c_rows.shape[0]
    tn = 1024
    return pl.pallas_call(
        _ada_kernel,
        grid=(depth, n // tn),
        in_specs=[
            pl.BlockSpec((rows, d), lambda l, j: (0, 0)),
            pl.BlockSpec((None, d, tn), lambda l, j: (l, 0, j)),
            pl.BlockSpec((None, 1, tn), lambda l, j: (l, 0, j)),
        ],
        out_specs=pl.BlockSpec((None, rows, tn), lambda l, j: (l, 0, j)),
        out_shape=jax.ShapeDtypeStruct((depth, rows, n), F32),
        compiler_params=_params("arbitrary", "arbitrary"),
        name="ada",
    )(c_rows, w_ada, b_ada.reshape(depth, 1, n))


def _inproj_kernel(x_ref, g_ref, mod_ref, w_ref, o_ref, h_ref):
    @pl.when(pl.program_id(1) == 0)
    def _():
        y = _rms(x_ref[...]) * g_ref[0:1, :]
        h = y * (1.0 + mod_ref[1:2, :]) + mod_ref[0:1, :]
        h_ref[...] = h.astype(BF16)

    o_ref[...] = jnp.dot(h_ref[...], w_ref[...], preferred_element_type=F32).astype(o_ref.dtype)


def _inproj(x, norm_g, mod, w, mod_span):
    t, d = x.shape
    n = w.shape[1]
    tm = min(1024, mod_span)
    tn = min(1024, n)
    per_seq = mod_span // tm
    return pl.pallas_call(
        _inproj_kernel,
        grid=(t // tm, n // tn),
        in_specs=[
            pl.BlockSpec((tm, d), lambda i, j: (i, 0)),
            pl.BlockSpec((4, d), lambda i, j: (0, 0)),
            pl.BlockSpec((None, 6, d), lambda i, j: (i // per_seq, 0, 0)),
            pl.BlockSpec((d, tn), lambda i, j: (0, j)),
        ],
        out_specs=pl.BlockSpec((tm, tn), lambda i, j: (i, j)),
        out_shape=jax.ShapeDtypeStruct((t, n), BF16),
        scratch_shapes=[pltpu.VMEM((tm, d), BF16)],
        compiler_params=_params("arbitrary", "arbitrary"),
        name="inproj",
    )(x, norm_g, mod, w)


def _rope(t, cos_ref, sin_ref):
    return t * cos_ref[...] + pltpu.roll(t, RET_DK // 2, 1) * sin_ref[...]


def _head_log_decay(dec_ref, direction, head):
    return _log_sigmoid(jnp.full((1, RET_DK), dec_ref[direction, head], F32))


def _ret_state_kernel(use_rope, n_chunks, *refs):
    if use_rope:
        (dec_ref, kf_ref, vf_ref, kb_ref, vb_ref, cf_ref, sf_ref, cb_ref, sb_ref, s0f_ref, s0b_ref,
         of_ref, ob_ref, ff_ref, fb_ref, st_ref) = refs
    else:
        (dec_ref, kf_ref, vf_ref, kb_ref, vb_ref, s0f_ref, s0b_ref,
         of_ref, ob_ref, ff_ref, fb_ref, st_ref) = refs
    i = pl.program_id(1)

    @pl.when(i == 0)
    def _():
        st_ref[0] = s0f_ref[...]
        st_ref[1] = s0b_ref[...]

    row = lax.broadcasted_iota(jnp.int32, (RET_CHUNK, RET_DK), 0).astype(F32)
    for h in range(RET_HEADS):
        hs = slice(h * RET_DK, (h + 1) * RET_DK)
        lg_f = _head_log_decay(dec_ref, 0, h)
        lg_b = _head_log_decay(dec_ref, 1, h)
        kf = kf_ref[:, hs].astype(F32)
        kb = kb_ref[:, hs].astype(F32)
        if use_rope:
            kf = _rope(kf, cf_ref, sf_ref)
            kb = _rope(kb, cb_ref, sb_ref)
        kf = kf * K_SCALE * jnp.exp(lg_f * (RET_CHUNK - 1.0 - row))
        kb = kb * K_SCALE * jnp.exp(lg_b * row)
        upd_f = jnp.dot(kf.T.astype(BF16), vf_ref[:, hs], preferred_element_type=F32)
        upd_b = jnp.dot(kb.T.astype(BF16), vb_ref[:, hs], preferred_element_type=F32)
        s_f = st_ref[0, h]
        s_b = st_ref[1, h]
        of_ref[h] = s_f.astype(of_ref.dtype)
        ob_ref[h] = s_b.astype(ob_ref.dtype)
        st_ref[0, h] = s_f * jnp.exp(lg_f * RET_CHUNK) + upd_f
        st_ref[1, h] = s_b * jnp.exp(lg_b * RET_CHUNK) + upd_b

    @pl.when(i == n_chunks - 1)
    def _():
        ff_ref[...] = st_ref[0]
        fb_ref[...] = st_ref[1]


def _ret_state(parts, dec, s0f, s0b, rope_tabs, batch, seq_len):
    n = seq_len // RET_CHUNK
    use_rope = rope_tabs is not None
    blk = lambda col, rev: pl.BlockSpec(
        (RET_CHUNK, BRANCH_W), (lambda b, i: (b * n + n - 1 - i, col)) if rev else (lambda b, i: (b * n + i, col)))
    tab = lambda rev: pl.BlockSpec((RET_CHUNK, RET_DK), (lambda b, i: (n - 1 - i, 0)) if rev else (lambda b, i: (i, 0)))
    st_in = pl.BlockSpec((None, RET_HEADS, RET_DK, RET_DK), lambda b, i: (b, 0, 0, 0))
    in_specs = [pl.BlockSpec(memory_space=pltpu.SMEM), blk(0, False), blk(1, False), blk(0, True), blk(1, True)]
    args = [dec, parts, parts, parts, parts]
    if use_rope:
        in_specs += [tab(False), tab(False), tab(True), tab(True)]
        args += [rope_tabs[0], rope_tabs[1], rope_tabs[0], rope_tabs[1]]
    in_specs += [st_in, st_in]
    args += [s0f, s0b]
    st_shape = jax.ShapeDtypeStruct((batch, n, RET_HEADS, RET_DK, RET_DK), BF16)
    fin_shape = jax.ShapeDtypeStruct((batch, RET_HEADS, RET_DK, RET_DK), F32)
    return pl.pallas_call(
        functools.partial(_ret_state_kernel, use_rope, n),
        grid=(batch, n),
        in_specs=in_specs,
        out_specs=[
            pl.BlockSpec((None, None, RET_HEADS, RET_DK, RET_DK), lambda b, i: (b, i, 0, 0, 0)),
            pl.BlockSpec((None, None, RET_HEADS, RET_DK, RET_DK), lambda b, i: (b, n - 1 - i, 0, 0, 0)),
            st_in, st_in,
        ],
        out_shape=[st_shape, st_shape, fin_shape, fin_shape],
        scratch_shapes=[pltpu.VMEM((2, RET_HEADS, RET_DK, RET_DK), F32)],
        compiler_params=_params("arbitrary", "arbitrary"),
        name="ret_state",
    )(*args)


def _ret_out_kernel(use_rope, *refs):
    if use_rope:
        dec_ref, k_ref, v_ref, q_ref, g_ref, cos_ref, sin_ref, sf_ref, sb_ref, o_ref = refs
    else:
        dec_ref, k_ref, v_ref, q_ref, g_ref, sf_ref, sb_ref, o_ref = refs
    row = lax.broadcasted_iota(jnp.int32, (RET_CHUNK, RET_CHUNK), 0).astype(F32)
    col = lax.broadcasted_iota(jnp.int32, (RET_CHUNK, RET_CHUNK), 1).astype(F32)
    for h in range(RET_HEADS):
        hs = slice(h * RET_DK, (h + 1) * RET_DK)
        lg_f = _head_log_decay(dec_ref, 0, h)
        lg_b = _head_log_decay(dec_ref, 1, h)
        q = q_ref[:, hs].astype(F32)
        k = k_ref[:, hs].astype(F32)
        if use_rope:
            q = _rope(q, cos_ref, sin_ref)
            k = _rope(k, cos_ref, sin_ref)
        k = k * K_SCALE
        scores = lax.dot_general(q.astype(BF16), k.astype(BF16), (((1,), (1,)), ((), ())),
                                 preferred_element_type=F32)
        decay = jnp.where(row >= col, jnp.exp(lg_f * jnp.maximum(row - col, 0.0)),
                          jnp.exp(lg_b * jnp.maximum(col - row, 0.0)))
        y = jnp.dot((scores * decay).astype(BF16), v_ref[:, hs], preferred_element_type=F32)
        q_f = q * jnp.exp(lg_f * (row + 1.0))
        q_b = q * jnp.exp(lg_b * (RET_CHUNK - row))
        y = y + jnp.dot(q_f.astype(BF16), sf_ref[h], preferred_element_type=F32)
        y = y + jnp.dot(q_b.astype(BF16), sb_ref[h], preferred_element_type=F32)
        mu = jnp.mean(y, axis=-1, keepdims=True)
        yc = y - mu
        yn = yc * lax.rsqrt(jnp.mean(yc * yc, axis=-1, keepdims=True) + EPS)
        g = g_ref[:, hs].astype(F32)
        o_ref[:, hs] = (yn * (g * _sigmoid(g))).astype(o_ref.dtype)


def _ret_out(parts, dec, st_f, st_b, rope_tabs, batch, seq_len):
    n = seq_len // RET_CHUNK
    use_rope = rope_tabs is not None
    blk = lambda col: pl.BlockSpec((RET_CHUNK, BRANCH_W), lambda b, i: (b * n + i, col))
    st = pl.BlockSpec((None, None, RET_HEADS, RET_DK, RET_DK), lambda b, i: (b, i, 0, 0, 0))
    in_specs = [pl.BlockSpec(memory_space=pltpu.SMEM), blk(0), blk(1), blk(2), blk(3)]
    args = [dec, parts, parts, parts, parts]
    if use_rope:
        tab = pl.BlockSpec((RET_CHUNK, RET_DK), lambda b, i: (i, 0))
        in_specs += [tab, tab]
        args += list(rope_tabs)
    in_specs += [st, st]
    args += [st_f, st_b]
    return pl.pallas_call(
        functools.partial(_ret_out_kernel, use_rope),
        grid=(batch, n),
        in_specs=in_specs,
        out_specs=blk(0),
        out_shape=jax.ShapeDtypeStruct((batch * seq_len, BRANCH_W), BF16),
        compiler_params=_params("arbitrary", "arbitrary"),
        name="ret_out",
    )(*args)


def _chan_dft_kernel(x_ref, m_ref, o_ref):
    for g in range(FNET_GROUPS):
        r = jnp.dot(x_ref[:, g * FNET_GW:(g + 1) * FNET_GW], m_ref[...], preferred_element_type=F32)
        o_ref[:, g * FNET_GW:(g + 1) * FNET_GW] = r[:, :FNET_GW].astype(o_ref.dtype)
        o_ref[:, BRANCH_W + g * FNET_GW:BRANCH_W + (g + 1) * FNET_GW] = r[:, FNET_GW:].astype(o_ref.dtype)


def _chan_dft(parts, chan_mat):
    t = parts.shape[0]
    tm = min(1024, t)
    return pl.pallas_call(
        _chan_dft_kernel,
        grid=(t // tm,),
        in_specs=[pl.BlockSpec((tm, BRANCH_W), lambda i: (i, 4)),
                  pl.BlockSpec((FNET_GW, 2 * FNET_GW), lambda i: (0, 0))],
        out_specs=pl.BlockSpec((tm, 2 * BRANCH_W), lambda i: (i, 0)),
        out_shape=jax.ShapeDtypeStruct((t, 2 * BRANCH_W), BF16),
        compiler_params=_params("arbitrary"),
        name="chan_dft",
    )(parts, chan_mat)


def _fft_rows_kernel(rows, cols_per_step, x_ref, f_ref, tc_ref, ts_ref, o_ref):
    res = jnp.dot(f_ref[...], x_ref[...], preferred_element_type=F32)
    for c in range(cols_per_step):
        tc = tc_ref[c]
        ts = ts_ref[c]
        for g in range(FNET_GROUPS):
            u0 = c * 2 * BRANCH_W + g * FNET_GW
            v0 = u0 + BRANCH_W
            a_re = res[:rows, u0:u0 + FNET_GW] - res[rows:, v0:v0 + FNET_GW]
            a_im = -(res[:rows, v0:v0 + FNET_GW] + res[rows:, u0:u0 + FNET_GW])
            gs = slice(g * FNET_GW, (g + 1) * FNET_GW)
            o_ref[0, c, :, gs] = (a_re * tc + a_im * ts).astype(o_ref.dtype)
            o_ref[1, c, :, gs] = (a_im * tc - a_re * ts).astype(o_ref.dtype)


def _fft_rows(uv, f_mat, tw_cos, tw_sin, batch, rows):
    cols_per_step = 8
    x = uv.reshape(batch, rows, GRID_W * 2 * BRANCH_W)
    return pl.pallas_call(
        functools.partial(_fft_rows_kernel, rows, cols_per_step),
        grid=(batch, GRID_W // cols_per_step),
        in_specs=[
            pl.BlockSpec((None, rows, cols_per_step * 2 * BRANCH_W), lambda b, j: (b, 0, j)),
            pl.BlockSpec((2 * rows, rows), lambda b, j: (0, 0)),
            pl.BlockSpec((cols_per_step, rows, FNET_GW), lambda b, j: (j, 0, 0)),
            pl.BlockSpec((cols_per_step, rows, FNET_GW), lambda b, j: (j, 0, 0)),
        ],
        out_specs=pl.BlockSpec((None, 2, cols_per_step, rows, BRANCH_W), lambda b, j: (b, 0, j, 0, 0)),
        out_shape=jax.ShapeDtypeStruct((batch, 2, GRID_W, rows, BRANCH_W), BF16),
        compiler_params=_params("arbitrary", "arbitrary"),
        name="fft_rows",
    )(x, f_mat, tw_cos, tw_sin)


def _fft_cols_kernel(scale, p_ref, f_ref, o_ref):
    n = p_ref.shape[-1]
    stacked = p_ref[...].reshape(2 * GRID_W, n)
    o_ref[...] = (jnp.dot(f_ref[...], stacked, preferred_element_type=F32) * scale).astype(o_ref.dtype)


def _fft_cols(p, f_mat, batch, rows, scale):
    k1_per_step = min(8, rows)
    pv = p.reshape(batch, 2, GRID_W, rows * BRANCH_W)
    out = pl.pallas_call(
        functools.partial(_fft_cols_kernel, scale),
        grid=(batch, rows // k1_per_step),
        in_specs=[
            pl.BlockSpec((None, 2, GRID_W, k1_per_step * BRANCH_W), lambda b, j: (b, 0, 0, j)),
            pl.BlockSpec((GRID_W, 2 * GRID_W), lambda b, j: (0, 0)),
        ],
        out_specs=pl.BlockSpec((None, GRID_W, k1_per_step * BRANCH_W), lambda b, j: (b, 0, j)),
        out_shape=jax.ShapeDtypeStruct((batch, GRID_W, rows * BRANCH_W), BF16),
        compiler_params=_params("arbitrary", "arbitrary"),
        name="fft_cols",
    )(pv, f_mat)
    return out.reshape(batch * GRID_W * rows, BRANCH_W)


def _fnet_dense_kernel(scale, uv_ref, c_ref, s_ref, o_ref):
    y = jnp.dot(c_ref[...], uv_ref[:, :BRANCH_W], preferred_element_type=F32)
    y = y - jnp.dot(s_ref[...], uv_ref[:, BRANCH_W:], preferred_element_type=F32)
    o_ref[...] = (y * scale).astype(o_ref.dtype)


def _fnet_dense(uv, cos_mat, sin_mat, batch, seq_len, scale):
    return pl.pallas_call(
        functools.partial(_fnet_dense_kernel, scale),
        grid=(batch,),
        in_specs=[pl.BlockSpec((seq_len, 2 * BRANCH_W), lambda b: (b, 0)),
                  pl.BlockSpec((seq_len, seq_len), lambda b: (0, 0)),
                  pl.BlockSpec((seq_len, seq_len), lambda b: (0, 0))],
        out_specs=pl.BlockSpec((seq_len, BRANCH_W), lambda b: (b, 0)),
        out_shape=jax.ShapeDtypeStruct((batch * seq_len, BRANCH_W), BF16),
        compiler_params=_params("arbitrary"),
        name="fnet_dense",
    )(uv, cos_mat, sin_mat)


def _dft_cos_sin(n):
    idx = np.arange(n)
    ang = 2.0 * np.pi * ((np.outer(idx, idx)) % n) / n
    return np.cos(ang), np.sin(ang)


def _mxu_const(a):
    return jnp.asarray(a, F32).astype(BF16)


def _fourier_mix(parts, batch, seq_len):
    scale = 1.0 / math.sqrt(seq_len * FNET_GW)
    rows = seq_len // GRID_W
    c_ch, s_ch = _dft_cos_sin(FNET_GW)
    uv = _chan_dft(parts, _mxu_const(np.concatenate([c_ch, s_ch], axis=1)))
    if rows >= 16:
        c_r, s_r = _dft_cos_sin(rows)
        row_mat = _mxu_const(np.concatenate([c_r, s_r], axis=0))
        c_c, s_c = _dft_cos_sin(GRID_W)
        col_mat = _mxu_const(np.concatenate([c_c, s_c], axis=1))
        ang = 2.0 * np.pi * np.outer(np.arange(GRID_W), np.arange(rows)) / seq_len
        tw_cos = jnp.asarray(np.broadcast_to(np.cos(ang)[:, :, None], (GRID_W, rows, FNET_GW)), F32)
        tw_sin = jnp.asarray(np.broadcast_to(np.sin(ang)[:, :, None], (GRID_W, rows, FNET_GW)), F32)
        p = _fft_rows(uv, row_mat, tw_cos, tw_sin, batch, rows)
        return _fft_cols(p, col_mat, batch, rows, scale)
    c_l, s_l = _dft_cos_sin(seq_len)
    return _fnet_dense(uv, _mxu_const(c_l), _mxu_const(s_l), batch, seq_len, scale)


def _conv_kernel(tiles_per_seq, tl, scb_ref, scc_ref, scx_ref, cfa_ref, cfb_ref,
                 scc_p, scx_p, cfa_p, cfb_p, scc_n, scx_n, cfa_n, cfb_n,
                 wsc_ref, wcf_ref, ln_ref, osc_ref, ocf_ref, esc_ref, ecf_ref):
    i = pl.program_id(0)
    keep_prev = jnp.where(i % tiles_per_seq == 0, 0.0, 1.0)
    keep_next = jnp.where(i % tiles_per_seq == tiles_per_seq - 1, 0.0, 1.0)

    def gated(a_ref, b_ref):
        return a_ref[...].astype(F32) * _sigmoid(b_ref[...].astype(F32))

    def prod(a_ref, b_ref):
        return a_ref[...].astype(F32) * b_ref[...].astype(F32)

    esc_ref[0:HALO_ROWS] = prod(scc_p, scx_p) * keep_prev
    esc_ref[HALO_ROWS:HALO_ROWS + tl] = prod(scc_ref, scx_ref)
    esc_ref[HALO_ROWS + tl:] = prod(scc_n, scx_n) * keep_next
    ecf_ref[0:HALO_ROWS] = gated(cfa_p, cfb_p) * keep_prev
    ecf_ref[HALO_ROWS:HALO_ROWS + tl] = gated(cfa_ref, cfb_ref)
    ecf_ref[HALO_ROWS + tl:] = gated(cfa_n, cfb_n) * keep_next

    rs = 32
    lane_groups = BRANCH_W // 128
    for r in range(tl // rs):
        r0 = r * rs
        accs = []
        for lg in range(lane_groups):
            ls = slice(lg * 128, (lg + 1) * 128)
            acc = jnp.zeros((rs, 128), F32)
            for j in range(SC_WIDTH):
                off = HALO_ROWS + r0 + j - (SC_WIDTH - 1) // 2
                acc = acc + wsc_ref[j:j + 1, ls] * esc_ref[off:off + rs, ls]
            osc_ref[r0:r0 + rs, ls] = (scb_ref[r0:r0 + rs, ls].astype(F32) * acc).astype(osc_ref.dtype)
            acc = jnp.zeros((rs, 128), F32)
            for j in range(CF_WIDTH):
                off = HALO_ROWS + r0 + j - (CF_WIDTH - 1) // 2
                acc = acc + wcf_ref[j:j + 1, ls] * ecf_ref[off:off + rs, ls]
            accs.append(acc)
        tot = accs[0]
        for a in accs[1:]:
            tot = tot + a
        mu = jnp.sum(tot, axis=-1, keepdims=True) * (1.0 / BRANCH_W)
        sq = None
        for a in accs:
            d = a - mu
            sq = d * d if sq is None else sq + d * d
        inv = lax.rsqrt(jnp.sum(sq, axis=-1, keepdims=True) * (1.0 / BRANCH_W) + EPS)
        for lg, a in enumerate(accs):
            ls = slice(lg * 128, (lg + 1) * 128)
            z = (a - mu) * inv * ln_ref[0:1, ls] + ln_ref[1:2, ls]
            ocf_ref[r0:r0 + rs, ls] = (z * _sigmoid(z)).astype(ocf_ref.dtype)


def _conv_branches(parts, sc_conv, cf_conv, cf_ln, seq_len):
    t = parts.shape[0]
    tl = min(256, seq_len)
    assert seq_len % tl == 0 and tl % HALO_ROWS == 0
    tiles_per_seq = seq_len // tl
    halo_per_tile = tl // HALO_ROWS
    n_halo = t // HALO_ROWS
    main = lambda col: pl.BlockSpec((tl, BRANCH_W), lambda i: (i, col))
    prev = lambda col: pl.BlockSpec((HALO_ROWS, BRANCH_W), lambda i: (jnp.maximum(i * halo_per_tile - 1, 0), col))
    nxt = lambda col: pl.BlockSpec((HALO_ROWS, BRANCH_W),
                                   lambda i: (jnp.minimum((i + 1) * halo_per_tile, n_halo - 1), col))
    full = lambda a: pl.BlockSpec(a.shape, lambda i: (0, 0))
    out = pl.BlockSpec((tl, BRANCH_W), lambda i: (i, 0))
    return pl.pallas_call(
        functools.partial(_conv_kernel, tiles_per_seq, tl),
        grid=(t // tl,),
        in_specs=[main(5), main(6), main(7), main(8), main(9),
                  prev(6), prev(7), prev(8), prev(9), nxt(6), nxt(7), nxt(8), nxt(9),
                  full(sc_conv), full(cf_conv), full(cf_ln)],
        out_specs=[out, out],
        out_shape=[jax.ShapeDtypeStruct((t, BRANCH_W), BF16)] * 2,
        scratch_shapes=[pltpu.VMEM((tl + 2 * HALO_ROWS, BRANCH_W), F32)] * 2,
        compiler_params=_params("arbitrary"),
        name="conv_branches",
    )(*([parts] * 13), sc_conv, cf_conv, cf_ln)


def _merge_kernel(y0_ref, y1_ref, y2_ref, y3_ref, g0_ref, g1_ref, g2_ref, g3_ref, w_ref, o_ref):
    acc = None
    for n, (y_ref, g_ref) in enumerate(((y0_ref, g0_ref), (y1_ref, g1_ref), (y2_ref, g2_ref), (y3_ref, g3_ref))):
        p = jnp.dot(y_ref[...], w_ref[n], preferred_element_type=F32)
        term = _sigmoid(g_ref[...].astype(F32)) * p
        acc = term if acc is None else acc + term
    o_ref[...] = acc.astype(o_ref.dtype)


def _merge(ys, parts, w_branch):
    t = parts.shape[0]
    tm = min(512, t)
    tn = 512
    per_branch = D_MODEL // tn
    gate0 = GATE_COL0 // tn
    y_spec = pl.BlockSpec((tm, BRANCH_W), lambda i, j: (i, 0))
    gate = lambda n: pl.BlockSpec((tm, tn), lambda i, j: (i, gate0 + n * per_branch + j))
    return pl.pallas_call(
        _merge_kernel,
        grid=(t // tm, per_branch),
        in_specs=[y_spec] * 4 + [gate(0), gate(1), gate(2), gate(3),
                                 pl.BlockSpec((N_BRANCH, BRANCH_W, tn), lambda i, j: (0, 0, j))],
        out_specs=pl.BlockSpec((tm, tn), lambda i, j: (i, j)),
        out_shape=jax.ShapeDtypeStruct((t, D_MODEL), BF16),
        compiler_params=_params("arbitrary", "arbitrary"),
        name="merge",
    )(*ys, parts, parts, parts, parts, w_branch)


def _outproj_kernel(m_ref, w_ref, x_ref, g_ref, mod_ref, xo_ref, h_ref):
    mix = jnp.dot(m_ref[...], w_ref[...], preferred_element_type=F32)
    x1 = x_ref[...] + mod_ref[2:3, :] * (_rms(mix) * g_ref[1:2, :])
    xo_ref[...] = x1
    h = _rms(x1) * g_ref[2:3, :]
    h_ref[...] = (h * (1.0 + mod_ref[4:5, :]) + mod_ref[3:4, :]).astype(h_ref.dtype)


def _outproj(merged, w_out, x, norm_g, mod, mod_span):
    t, d = x.shape
    tm = min(512, mod_span)
    per_seq = mod_span // tm
    return pl.pallas_call(
        _outproj_kernel,
        grid=(t // tm,),
        in_specs=[
            pl.BlockSpec((tm, d), lambda i: (i, 0)),
            pl.BlockSpec((d, d), lambda i: (0, 0)),
            pl.BlockSpec((tm, d), lambda i: (i, 0)),
            pl.BlockSpec((4, d), lambda i: (0, 0)),
            pl.BlockSpec((None, 6, d), lambda i: (i // per_seq, 0, 0)),
        ],
        out_specs=[pl.BlockSpec((tm, d), lambda i: (i, 0)), pl.BlockSpec((tm, d), lambda i: (i, 0))],
        out_shape=[jax.ShapeDtypeStruct((t, d), F32), jax.ShapeDtypeStruct((t, d), BF16)],
        compiler_params=_params("arbitrary"),
        name="outproj",
    )(merged, w_out, x, norm_g, mod)


def _ffn_kernel(n_f, h_ref, w1_ref, w2_ref, x_ref, g_ref, mod_ref, o_ref, acc_ref):
    f = pl.program_id(1)

    @pl.when(f == 0)
    def _():
        acc_ref[...] = jnp.zeros_like(acc_ref)

    a = jnp.maximum(jnp.dot(h_ref[...], w1_ref[...], preferred_element_type=F32), 0.0)
    acc_ref[...] += jnp.dot((a * a).astype(BF16), w2_ref[...], preferred_element_type=F32)

    @pl.when(f == n_f - 1)
    def _():
        o_ref[...] = x_ref[...] + mod_ref[5:6, :] * (_rms(acc_ref[...]) * g_ref[3:4, :])


def _ffn(h, w1, w2, x, norm_g, mod, mod_span):
    t, d = x.shape
    d_ff = w1.shape[1]
    tm = min(512, mod_span)
    tf = 512
    per_seq = mod_span // tm
    n_f = d_ff // tf
    return pl.pallas_call(
        functools.partial(_ffn_kernel, n_f),
        grid=(t // tm, n_f),
        in_specs=[
            pl.BlockSpec((tm, d), lambda i, f: (i, 0)),
            pl.BlockSpec((d, tf), lambda i, f: (0, f)),
            pl.BlockSpec((tf, d), lambda i, f: (f, 0)),
            pl.BlockSpec((tm, d), lambda i, f: (i, 0)),
            pl.BlockSpec((4, d), lambda i, f: (0, 0)),
            pl.BlockSpec((None, 6, d), lambda i, f: (i // per_seq, 0, 0)),
        ],
        out_specs=pl.BlockSpec((tm, d), lambda i, f: (i, 0)),
        out_shape=jax.ShapeDtypeStruct((t, d), F32),
        scratch_shapes=[pltpu.VMEM((tm, d), F32)],
        compiler_params=_params("arbitrary", "arbitrary"),
        name="ffn",
    )(h, w1, w2, x, norm_g, mod)


def _rope_tables(seq_len):
    rows = seq_len // GRID_W
    row = jnp.repeat(jnp.arange(rows, dtype=F32), GRID_W)
    col = jnp.tile(jnp.arange(GRID_W, dtype=F32), rows)
    freqs = ROPE_BASE ** (-jnp.arange(ROPE_PAIRS, dtype=F32) / ROPE_PAIRS)
    ang = jnp.concatenate([row[:, None] * freqs[None], col[:, None] * freqs[None]], axis=-1)
    cos, sin = jnp.cos(ang), jnp.sin(ang)
    return jnp.concatenate([cos, cos], axis=-1), jnp.concatenate([-sin, sin], axis=-1)


def _mixer_tail(parts, y_ret, x, mod, norm_g, sc_conv, cf_conv, cf_ln, w_branch, w_out, w_ff1, w_ff2,
                batch, seq_len, mod_span):
    y_f = _fourier_mix(parts, batch, seq_len)
    y_sc, y_cf = _conv_branches(parts, sc_conv, cf_conv, cf_ln, seq_len)
    merged = _merge((y_ret, y_f, y_sc, y_cf), parts, w_branch)
    x1, h2 = _outproj(merged, w_out, x, norm_g, mod, mod_span)
    return _ffn(h2, w_ff1, w_ff2, x1, norm_g, mod, mod_span)


def kernel(x, c, ctx, c_ctx, w_ada, b_ada, norm_g, w_in, ret_decay, sc_conv, cf_conv, cf_ln,
           w_branch, w_out, w_ff1, w_ff2):
    batch, seq_len, d = x.shape
    ctx_len = ctx.shape[1]
    depth = w_ada.shape[0]
    assert d == D_MODEL and seq_len % (GRID_W * 16) == 0 and ctx_len % RET_CHUNK == 0

    n_rows = -(-(batch + 1) // 8) * 8
    c_rows = jnp.concatenate([c, c_ctx[None], jnp.zeros((n_rows - batch - 1, d), F32)], axis=0)
    mod_all = _ada(c_rows, w_ada, b_ada).reshape(depth, n_rows, 6, d)

    rope_tabs = _rope_tables(seq_len)
    xf = x.reshape(batch * seq_len, d)
    xc = ctx.reshape(batch * ctx_len, d)
    zeros_state = jnp.zeros((batch, RET_HEADS, RET_DK, RET_DK), F32)

    for layer in range(depth):
        with_ctx = layer < depth - 1
        mod = mod_all[layer, :batch]
        mod_c = mod_all[layer, batch:batch + 1]
        g = norm_g[layer]
        dec = ret_decay[layer].astype(F32)
        w_in_l = w_in[layer].astype(BF16)
        wb = w_branch[layer].astype(BF16)
        wo = w_out[layer].astype(BF16)
        w1 = w_ff1[layer].astype(BF16)
        w2 = w_ff2[layer].astype(BF16)

        w_in_c = w_in_l if with_ctx else w_in_l[:, :2 * BRANCH_W]
        parts_c = _inproj(xc, g, mod_c, w_in_c, batch * ctx_len)
        stc_f, stc_b, s0f, s0b = _ret_state(parts_c, dec, zeros_state, zeros_state, None, batch, ctx_len)

        parts = _inproj(xf, g, mod, w_in_l, seq_len)
        st_f, st_b, _, _ = _ret_state(parts, dec, s0f, s0b, rope_tabs, batch, seq_len)
        y_ret = _ret_out(parts, dec, st_f, st_b, rope_tabs, batch, seq_len)
        xf = _mixer_tail(parts, y_ret, xf, mod, g, sc_conv[layer], cf_conv[layer], cf_ln[layer],
                         wb, wo, w1, w2, batch, seq_len, seq_len)

        if with_ctx:
            y_ret_c = _ret_out(parts_c, dec, stc_f, stc_b, None, batch, ctx_len)
            xc = _mixer_tail(parts_c, y_ret_c, xc, mod_c, g, sc_conv[layer], cf_conv[layer], cf_ln[layer],
                             wb, wo, w1, w2, batch, ctx_len, batch * ctx_len)
    return xf.reshape(batch, seq_len, d)
```

```python
import functools
import math

import numpy as np
import jax
import jax.numpy as jnp
from jax import lax
from jax.experimental import pallas as pl
from jax.experimental.pallas import tpu as pltpu

F32 = jnp.float32
BF16 = jnp.bfloat16

D_MODEL = 2048
GRID_W = 64
N_BRANCH = 4
BRANCH_W = D_MODEL // 4
RET_HEADS = 4
RET_DK = BRANCH_W // RET_HEADS
RET_CHUNK = 128
ROPE_PAIRS = RET_DK // 4
ROPE_BASE = 10000.0
FNET_GROUPS = 4
FNET_GW = BRANCH_W // FNET_GROUPS
SC_WIDTH = 3
CF_WIDTH = 31
GATE_COL0 = 10 * BRANCH_W
EPS = 1e-6
K_SCALE = RET_DK ** -0.5

VMEM_LIMIT_BYTES = 56 * 1024 * 1024
SUBLANES = 8
HALO_ROWS = 16
LANES = 128


def _params(*sem):
    return pltpu.CompilerParams(dimension_semantics=sem, vmem_limit_bytes=VMEM_LIMIT_BYTES)


def _sigmoid(v):
    return 0.5 * jnp.tanh(0.5 * v) + 0.5


def _log_sigmoid(v):
    return jnp.minimum(v, 0.0) - jnp.log1p(jnp.exp(-jnp.abs(v)))


def _rms(v):
    return v * lax.rsqrt(jnp.mean(v * v, axis=-1, keepdims=True) + EPS)


def _ada_kernel(c_ref, w_ref, b_ref, o_ref):
    c = c_ref[...]
    s = c * _sigmoid(c)
    o_ref[...] = jnp.dot(s, w_ref[...], preferred_element_type=F32,
                         precision=lax.Precision.HIGHEST) + b_ref[...]


def _ada(c_rows, w_ada, b_ada):
    depth, d, n = w_ada.shape
    rows = c_rows.shape[0]
    tn = 1024
    return pl.pallas_call(
        _ada_kernel,
        grid=(depth, n // tn),
        in_specs=[
            pl.BlockSpec((rows, d), lambda l, j: (0, 0)),
            pl.BlockSpec((None, d, tn), lambda l, j: (l, 0, j)),
            pl.BlockSpec((None, 1, tn), lambda l, j: (l, 0, j)),
        ],
        out_specs=pl.BlockSpec((None, rows, tn), lambda l, j: (l, 0, j)),
        out_shape=jax.ShapeDtypeStruct((depth, rows, n), F32),
        compiler_params=_params("arbitrary", "arbitrary"),
        name="ada",
    )(c_rows, w_ada, b_ada.reshape(depth, 1, n))


def _inproj_kernel(x_ref, g_ref, mod_ref, w_ref, o_ref, h_ref):
    @pl.when(pl.program_id(1) == 0)
    def _():
        y = _rms(x_ref[...]) * g_ref[0:1, :]
        h = y * (1.0 + mod_ref[1:2, :]) + mod_ref[0:1, :]
        h_ref[...] = h.astype(BF16)

    o_ref[...] = jnp.dot(h_ref[...], w_ref[...], preferred_element_type=F32).astype(o_ref.dtype)


def _inproj(x, norm_g, mod, w, layer, n_cols, mod_span):
    t, d = x.shape
    tm = min(1024, mod_span)
    tn = min(1024, n_cols)
    per_seq = mod_span // tm
    return pl.pallas_call(
        _inproj_kernel,
        grid=(t // tm, n_cols // tn),
        in_specs=[
            pl.BlockSpec((tm, d), lambda i, j: (i, 0)),
            pl.BlockSpec((None, 4, d), lambda i, j: (layer, 0, 0)),
            pl.BlockSpec((None, 6, d), lambda i, j: (i // per_seq, 0, 0)),
            pl.BlockSpec((None, d, tn), lambda i, j: (layer, 0, j)),
        ],
        out_specs=pl.BlockSpec((tm, tn), lambda i, j: (i, j)),
        out_shape=jax.ShapeDtypeStruct((t, n_cols), BF16),
        scratch_shapes=[pltpu.VMEM((tm, d), BF16)],
        compiler_params=_params("arbitrary", "arbitrary"),
        name="inproj",
    )(x, norm_g, mod, w)


def _ret_tables_kernel(dec_ref, dmat_ref, qdec_ref, kdec_ref, cdec_ref):
    row = lax.broadcasted_iota(jnp.int32, (RET_CHUNK, RET_CHUNK), 0).astype(F32)
    col = lax.broadcasted_iota(jnp.int32, (RET_CHUNK, RET_CHUNK), 1).astype(F32)
    for h in range(RET_HEADS):
        lg_f = _log_sigmoid(jnp.full((1, RET_DK), dec_ref[0, h], F32))
        lg_b = _log_sigmoid(jnp.full((1, RET_DK), dec_ref[1, h], F32))
        dmat_ref[h] = K_SCALE * jnp.where(row >= col, jnp.exp(lg_f * jnp.maximum(row - col, 0.0)),
                                          jnp.exp(lg_b * jnp.maximum(col - row, 0.0)))
        qdec_ref[0, h] = jnp.exp(lg_f * (row + 1.0))
        qdec_ref[1, h] = jnp.exp(lg_b * (RET_CHUNK - row))
        kdec_ref[0, h] = K_SCALE * jnp.exp(lg_f * (RET_CHUNK - 1.0 - row))
        kdec_ref[1, h] = K_SCALE * jnp.exp(lg_b * row)
        cdec_ref[0, h] = jnp.broadcast_to(jnp.exp(lg_f * RET_CHUNK), (SUBLANES, RET_DK))
        cdec_ref[1, h] = jnp.broadcast_to(jnp.exp(lg_b * RET_CHUNK), (SUBLANES, RET_DK))


def _ret_tables(dec):
    sq = (RET_HEADS, RET_CHUNK, RET_CHUNK)
    return pl.pallas_call(
        _ret_tables_kernel,
        in_specs=[pl.BlockSpec(memory_space=pltpu.SMEM)],
        out_shape=[jax.ShapeDtypeStruct(sq, F32), jax.ShapeDtypeStruct((2,) + sq, F32),
                   jax.ShapeDtypeStruct((2,) + sq, F32),
                   jax.ShapeDtypeStruct((2, RET_HEADS, SUBLANES, RET_DK), F32)],
        name="ret_tables",
    )(dec)


def _rope(t, cos, sin):
    return t * cos + pltpu.roll(t, RET_DK // 2, 1) * sin


def _ret_state_kernel(use_rope, cps, n_steps, *refs):
    if use_rope:
        (kdec_ref, cdec_ref, kf_ref, vf_ref, kb_ref, vb_ref, cf_ref, sf_ref, cb_ref, sb_ref, s0f_ref, s0b_ref,
         of_ref, ob_ref, ff_ref, fb_ref, st_ref) = refs
    else:
        (kdec_ref, cdec_ref, kf_ref, vf_ref, kb_ref, vb_ref, s0f_ref, s0b_ref,
         of_ref, ob_ref, ff_ref, fb_ref, st_ref) = refs
    i = pl.program_id(1)

    @pl.when(i == 0)
    def _():
        st_ref[0] = s0f_ref[...]
        st_ref[1] = s0b_ref[...]

    for cc in range(cps):
        rf = slice(cc * RET_CHUNK, (cc + 1) * RET_CHUNK)
        rb = slice((cps - 1 - cc) * RET_CHUNK, (cps - cc) * RET_CHUNK)
        for h in range(RET_HEADS):
            hs = slice(h * RET_DK, (h + 1) * RET_DK)
            kf = kf_ref[rf, hs].astype(F32)
            kb = kb_ref[rb, hs].astype(F32)
            if use_rope:
                kf = _rope(kf, cf_ref[rf, :], sf_ref[rf, :])
                kb = _rope(kb, cb_ref[rb, :], sb_ref[rb, :])
            kf = kf * kdec_ref[0, h]
            kb = kb * kdec_ref[1, h]
            upd_f = jnp.dot(kf.T.astype(BF16), vf_ref[rf, hs], preferred_element_type=F32)
            upd_b = jnp.dot(kb.T.astype(BF16), vb_ref[rb, hs], preferred_element_type=F32)
            s_f = st_ref[0, h]
            s_b = st_ref[1, h]
            of_ref[cc, h] = s_f.astype(of_ref.dtype)
            ob_ref[cps - 1 - cc, h] = s_b.astype(ob_ref.dtype)
            st_ref[0, h] = s_f * cdec_ref[0, h, 0:1, :] + upd_f
            st_ref[1, h] = s_b * cdec_ref[1, h, 0:1, :] + upd_b

    @pl.when(i == n_steps - 1)
    def _():
        ff_ref[...] = st_ref[0]
        fb_ref[...] = st_ref[1]


def _chunks_per_step(n_chunks):
    return math.gcd(n_chunks, 4)


def _ret_state(parts, tables, s0f, s0b, rope_tabs, batch, seq_len):
    n = seq_len // RET_CHUNK
    cps = _chunks_per_step(n)
    ns = n // cps
    rows = cps * RET_CHUNK
    use_rope = rope_tabs is not None
    _, _, kdec, cdec = tables
    blk = lambda col, rev: pl.BlockSpec(
        (rows, BRANCH_W), (lambda b, i: (b * ns + ns - 1 - i, col)) if rev else (lambda b, i: (b * ns + i, col)))
    tab = lambda rev: pl.BlockSpec((rows, RET_DK), (lambda b, i: (ns - 1 - i, 0)) if rev else (lambda b, i: (i, 0)))
    st_in = pl.BlockSpec((None, RET_HEADS, RET_DK, RET_DK), lambda b, i: (b, 0, 0, 0))
    full = lambda a: pl.BlockSpec(a.shape, lambda b, i: (0,) * a.ndim)
    in_specs = [full(kdec), full(cdec), blk(0, False), blk(1, False), blk(0, True), blk(1, True)]
    args = [kdec, cdec, parts, parts, parts, parts]
    if use_rope:
        in_specs += [tab(False), tab(False), tab(True), tab(True)]
        args += [rope_tabs[0], rope_tabs[1], rope_tabs[0], rope_tabs[1]]
    in_specs += [st_in, st_in]
    args += [s0f, s0b]
    st_shape = jax.ShapeDtypeStruct((batch, n, RET_HEADS, RET_DK, RET_DK), BF16)
    fin_shape = jax.ShapeDtypeStruct((batch, RET_HEADS, RET_DK, RET_DK), F32)
    return pl.pallas_call(
        functools.partial(_ret_state_kernel, use_rope, cps, ns),
        grid=(batch, ns),
        in_specs=in_specs,
        out_specs=[
            pl.BlockSpec((None, cps, RET_HEADS, RET_DK, RET_DK), lambda b, i: (b, i, 0, 0, 0)),
            pl.BlockSpec((None, cps, RET_HEADS, RET_DK, RET_DK), lambda b, i: (b, ns - 1 - i, 0, 0, 0)),
            st_in, st_in,
        ],
        out_shape=[st_shape, st_shape, fin_shape, fin_shape],
        scratch_shapes=[pltpu.VMEM((2, RET_HEADS, RET_DK, RET_DK), F32)],
        compiler_params=_params("arbitrary", "arbitrary"),
        name="ret_state",
    )(*args)


def _ret_out_kernel(use_rope, cps, *refs):
    if use_rope:
        dmat_ref, qdec_ref, k_ref, v_ref, q_ref, g_ref, cos_ref, sin_ref, sf_ref, sb_ref, o_ref = refs
    else:
        dmat_ref, qdec_ref, k_ref, v_ref, q_ref, g_ref, sf_ref, sb_ref, o_ref = refs
    for cc in range(cps):
        rs = slice(cc * RET_CHUNK, (cc + 1) * RET_CHUNK)
        for h in range(RET_HEADS):
            hs = slice(h * RET_DK, (h + 1) * RET_DK)
            q = q_ref[rs, hs].astype(F32)
            k = k_ref[rs, hs].astype(F32)
            if use_rope:
                cos, sin = cos_ref[rs, :], sin_ref[rs, :]
                q = _rope(q, cos, sin)
                k = _rope(k, cos, sin)
            scores = lax.dot_general(q.astype(BF16), k.astype(BF16), (((1,), (1,)), ((), ())),
                                     preferred_element_type=F32)
            y = jnp.dot((scores * dmat_ref[h]).astype(BF16), v_ref[rs, hs], preferred_element_type=F32)
            y = y + jnp.dot((q * qdec_ref[0, h]).astype(BF16), sf_ref[cc, h], preferred_element_type=F32)
            y = y + jnp.dot((q * qdec_ref[1, h]).astype(BF16), sb_ref[cc, h], preferred_element_type=F32)
            mu = jnp.mean(y, axis=-1, keepdims=True)
            yc = y - mu
            yn = yc * lax.rsqrt(jnp.mean(yc * yc, axis=-1, keepdims=True) + EPS)
            g = g_ref[rs, hs].astype(F32)
            o_ref[rs, hs] = (yn * (g * _sigmoid(g))).astype(o_ref.dtype)


def _ret_out(parts, tables, st_f, st_b, rope_tabs, batch, seq_len):
    n = seq_len // RET_CHUNK
    cps = _chunks_per_step(n)
    ns = n // cps
    rows = cps * RET_CHUNK
    use_rope = rope_tabs is not None
    dmat, qdec, _, _ = tables
    blk = lambda col: pl.BlockSpec((rows, BRANCH_W), lambda b, i: (b * ns + i, col))
    st = pl.BlockSpec((None, cps, RET_HEADS, RET_DK, RET_DK), lambda b, i: (b, i, 0, 0, 0))
    full = lambda a: pl.BlockSpec(a.shape, lambda b, i: (0,) * a.ndim)
    in_specs = [full(dmat), full(qdec), blk(0), blk(1), blk(2), blk(3)]
    args = [dmat, qdec, parts, parts, parts, parts]
    if use_rope:
        tab = pl.BlockSpec((rows, RET_DK), lambda b, i: (i, 0))
        in_specs += [tab, tab]
        args += list(rope_tabs)
    in_specs += [st, st]
    args += [st_f, st_b]
    return pl.pallas_call(
        functools.partial(_ret_out_kernel, use_rope, cps),
        grid=(batch, ns),
        in_specs=in_specs,
        out_specs=blk(0),
        out_shape=jax.ShapeDtypeStruct((batch * seq_len, BRANCH_W), BF16),
        compiler_params=_params("arbitrary", "arbitrary"),
        name="ret_out",
    )(*args)


def _chan_dft_kernel(x_ref, m_ref, o_ref):
    for g in range(FNET_GROUPS):
        r = jnp.dot(x_ref[:, g * FNET_GW:(g + 1) * FNET_GW], m_ref[...], preferred_element_type=F32)
        o_ref[:, g * FNET_GW:(g + 1) * FNET_GW] = r[:, :FNET_GW]
        o_ref[:, BRANCH_W + g * FNET_GW:BRANCH_W + (g + 1) * FNET_GW] = r[:, FNET_GW:]


def _chan_dft(parts, chan_mat):
    t = parts.shape[0]
    tm = min(1024, t)
    return pl.pallas_call(
        _chan_dft_kernel,
        grid=(t // tm,),
        in_specs=[pl.BlockSpec((tm, BRANCH_W), lambda i: (i, 4)),
                  pl.BlockSpec((FNET_GW, 2 * FNET_GW), lambda i: (0, 0))],
        out_specs=pl.BlockSpec((tm, 2 * BRANCH_W), lambda i: (i, 0)),
        out_shape=jax.ShapeDtypeStruct((t, 2 * BRANCH_W), F32),
        compiler_params=_params("arbitrary"),
        name="chan_dft",
    )(parts, chan_mat)


def _fft_rows_kernel(rows, cols_per_step, x_ref, f_ref, tc_ref, ts_ref, o_ref):
    n = rows * cols_per_step
    x2d = x_ref[...].reshape(n, 2 * BRANCH_W).astype(BF16)
    res = jnp.dot(f_ref[...], x2d, preferred_element_type=F32)
    tc = tc_ref[...]
    ts = ts_ref[...]
    for g in range(FNET_GROUPS):
        u0 = g * FNET_GW
        v0 = u0 + BRANCH_W
        a_re = res[:n, u0:u0 + FNET_GW] - res[n:, v0:v0 + FNET_GW]
        a_im = -(res[:n, v0:v0 + FNET_GW] + res[n:, u0:u0 + FNET_GW])
        gs = slice(g * FNET_GW, (g + 1) * FNET_GW)
        o_ref[0, :, :, gs] = (a_re * tc + a_im * ts).reshape(rows, cols_per_step, FNET_GW)
        o_ref[1, :, :, gs] = (a_im * tc - a_re * ts).reshape(rows, cols_per_step, FNET_GW)


def _fft_rows(uv, f_mat, tw_cos, tw_sin, batch, rows):
    cols_per_step = SUBLANES
    n = rows * cols_per_step
    x = uv.reshape(batch, rows, GRID_W, 2 * BRANCH_W)
    return pl.pallas_call(
        functools.partial(_fft_rows_kernel, rows, cols_per_step),
        grid=(batch, GRID_W // cols_per_step),
        in_specs=[
            pl.BlockSpec((None, rows, cols_per_step, 2 * BRANCH_W), lambda b, j: (b, 0, j, 0)),
            pl.BlockSpec((2 * n, n), lambda b, j: (0, 0)),
            pl.BlockSpec((None, n, FNET_GW), lambda b, j: (j, 0, 0)),
            pl.BlockSpec((None, n, FNET_GW), lambda b, j: (j, 0, 0)),
        ],
        out_specs=pl.BlockSpec((None, 2, rows, cols_per_step, BRANCH_W), lambda b, j: (b, 0, 0, j, 0)),
        out_shape=jax.ShapeDtypeStruct((batch, 2, rows, GRID_W, BRANCH_W), F32),
        compiler_params=_params("arbitrary", "arbitrary"),
        name="fft_rows",
    )(x, f_mat, tw_cos, tw_sin)


def _fft_cols_kernel(scale, k1_per_step, p_ref, f_ref, o_ref):
    for k1 in range(k1_per_step):
        stacked = jnp.concatenate([p_ref[0, k1], p_ref[1, k1]], axis=0).astype(BF16)
        o_ref[:, k1, :] = jnp.dot(f_ref[...], stacked, preferred_element_type=F32) * scale


def _fft_cols(p, f_mat, batch, rows, scale):
    k1_per_step = SUBLANES
    out = pl.pallas_call(
        functools.partial(_fft_cols_kernel, scale, k1_per_step),
        grid=(batch, rows // k1_per_step),
        in_specs=[
            pl.BlockSpec((None, 2, k1_per_step, GRID_W, BRANCH_W), lambda b, j: (b, 0, j, 0, 0)),
            pl.BlockSpec((GRID_W, 2 * GRID_W), lambda b, j: (0, 0)),
        ],
        out_specs=pl.BlockSpec((None, GRID_W, k1_per_step, BRANCH_W), lambda b, j: (b, 0, j, 0)),
        out_shape=jax.ShapeDtypeStruct((batch, GRID_W, rows, BRANCH_W), F32),
        compiler_params=_params("arbitrary", "arbitrary"),
        name="fft_cols",
    )(p, f_mat)
    return out.reshape(batch * GRID_W * rows, BRANCH_W)


def _fnet_dense_kernel(scale, uv_ref, c_ref, s_ref, o_ref):
    y = jnp.dot(c_ref[...], uv_ref[:, :BRANCH_W].astype(BF16), preferred_element_type=F32)
    y = y - jnp.dot(s_ref[...], uv_ref[:, BRANCH_W:].astype(BF16), preferred_element_type=F32)
    o_ref[...] = y * scale


def _fnet_dense(uv, cos_mat, sin_mat, batch, seq_len, scale):
    return pl.pallas_call(
        functools.partial(_fnet_dense_kernel, scale),
        grid=(batch,),
        in_specs=[pl.BlockSpec((seq_len, 2 * BRANCH_W), lambda b: (b, 0)),
                  pl.BlockSpec((seq_len, seq_len), lambda b: (0, 0)),
                  pl.BlockSpec((seq_len, seq_len), lambda b: (0, 0))],
        out_specs=pl.BlockSpec((seq_len, BRANCH_W), lambda b: (b, 0)),
        out_shape=jax.ShapeDtypeStruct((batch * seq_len, BRANCH_W), F32),
        compiler_params=_params("arbitrary"),
        name="fnet_dense",
    )(uv, cos_mat, sin_mat)


def _dft_cos_sin(n):
    idx = np.arange(n)
    ang = 2.0 * np.pi * ((np.outer(idx, idx)) % n) / n
    return np.cos(ang), np.sin(ang)


def _mxu_const(a):
    return jnp.asarray(a, F32).astype(BF16)


def _fourier_mix(parts, batch, seq_len):
    scale = 1.0 / math.sqrt(seq_len * FNET_GW)
    rows = seq_len // GRID_W
    c_ch, s_ch = _dft_cos_sin(FNET_GW)
    uv = _chan_dft(parts, _mxu_const(np.concatenate([c_ch, s_ch], axis=1)))
    if rows % 16 == 0:
        c_r, s_r = _dft_cos_sin(rows)
        eye = np.eye(SUBLANES)
        row_mat = _mxu_const(np.concatenate([np.kron(c_r, eye), np.kron(s_r, eye)], axis=0))
        c_c, s_c = _dft_cos_sin(GRID_W)
        col_mat = _mxu_const(np.concatenate([c_c, s_c], axis=1))
        ang = 2.0 * np.pi * np.outer(np.arange(rows), np.arange(GRID_W)) / seq_len
        ang = ang.reshape(rows, GRID_W // SUBLANES, SUBLANES).transpose(1, 0, 2).reshape(-1, rows * SUBLANES)
        tw_cos = jnp.asarray(np.broadcast_to(np.cos(ang)[:, :, None], ang.shape + (FNET_GW,)), F32)
        tw_sin = jnp.asarray(np.broadcast_to(np.sin(ang)[:, :, None], ang.shape + (FNET_GW,)), F32)
        p = _fft_rows(uv, row_mat, tw_cos, tw_sin, batch, rows)
        return _fft_cols(p, col_mat, batch, rows, scale)
    c_l, s_l = _dft_cos_sin(seq_len)
    return _fnet_dense(uv, _mxu_const(c_l), _mxu_const(s_l), batch, seq_len, scale)


def _conv_kernel(tiles_per_seq, tl, scb_ref, scc_ref, scx_ref, cfa_ref, cfb_ref,
                 scc_p, scx_p, cfa_p, cfb_p, scc_n, scx_n, cfa_n, cfb_n,
                 wsc_ref, wcf_ref, ln_ref, osc_ref, ocf_ref, esc_ref, ecf_ref):
    i = pl.program_id(0)
    keep_prev = jnp.where(i % tiles_per_seq == 0, 0.0, 1.0)
    keep_next = jnp.where(i % tiles_per_seq == tiles_per_seq - 1, 0.0, 1.0)
    ext = tl + 2 * HALO_ROWS

    def gated(a_ref, b_ref):
        return a_ref[...].astype(F32) * _sigmoid(b_ref[...].astype(F32))

    def prod(a_ref, b_ref):
        return a_ref[...].astype(F32) * b_ref[...].astype(F32)

    esc_ref[0:HALO_ROWS] = prod(scc_p, scx_p) * keep_prev
    esc_ref[HALO_ROWS:HALO_ROWS + tl] = prod(scc_ref, scx_ref)
    esc_ref[HALO_ROWS + tl:] = prod(scc_n, scx_n) * keep_next
    ecf_ref[0, 0:HALO_ROWS] = gated(cfa_p, cfb_p) * keep_prev
    ecf_ref[0, HALO_ROWS:HALO_ROWS + tl] = gated(cfa_ref, cfb_ref)
    ecf_ref[0, HALO_ROWS + tl:ext] = gated(cfa_n, cfb_n) * keep_next
    for s in range(1, SUBLANES):
        ecf_ref[s, 0:ext - SUBLANES] = ecf_ref[0, s:ext - SUBLANES + s]

    rs = 32
    lane_groups = BRANCH_W // LANES
    for r in range(tl // rs):
        r0 = r * rs
        accs = []
        for lg in range(lane_groups):
            ls = slice(lg * LANES, (lg + 1) * LANES)
            acc = jnp.zeros((rs, LANES), F32)
            for j in range(SC_WIDTH):
                off = HALO_ROWS + r0 + j - (SC_WIDTH - 1) // 2
                acc = acc + wsc_ref[j:j + 1, ls] * esc_ref[off:off + rs, ls]
            osc_ref[r0:r0 + rs, ls] = (scb_ref[r0:r0 + rs, ls].astype(F32) * acc).astype(osc_ref.dtype)
            acc = jnp.zeros((rs, LANES), F32)
            for j in range(CF_WIDTH):
                off = HALO_ROWS + r0 + j - (CF_WIDTH - 1) // 2
                s = off % SUBLANES
                acc = acc + wcf_ref[j:j + 1, ls] * ecf_ref[s, off - s:off - s + rs, ls]
            accs.append(acc)
        tot = accs[0]
        for a in accs[1:]:
            tot = tot + a
        mu = jnp.sum(tot, axis=-1, keepdims=True) * (1.0 / BRANCH_W)
        sq = None
        for a in accs:
            d = a - mu
            sq = d * d if sq is None else sq + d * d
        inv = lax.rsqrt(jnp.sum(sq, axis=-1, keepdims=True) * (1.0 / BRANCH_W) + EPS)
        for lg, a in enumerate(accs):
            ls = slice(lg * LANES, (lg + 1) * LANES)
            z = (a - mu) * inv * ln_ref[0:1, ls] + ln_ref[1:2, ls]
            ocf_ref[r0:r0 + rs, ls] = (z * _sigmoid(z)).astype(ocf_ref.dtype)


def _conv_branches(parts, sc_conv, cf_conv, cf_ln, layer, seq_len):
    t = parts.shape[0]
    tl = min(256, seq_len)
    assert seq_len % tl == 0 and tl % HALO_ROWS == 0
    tiles_per_seq = seq_len // tl
    halo_per_tile = tl // HALO_ROWS
    n_halo = t // HALO_ROWS
    main = lambda col: pl.BlockSpec((tl, BRANCH_W), lambda i: (i, col))
    prev = lambda col: pl.BlockSpec((HALO_ROWS, BRANCH_W), lambda i: (jnp.maximum(i * halo_per_tile - 1, 0), col))
    nxt = lambda col: pl.BlockSpec((HALO_ROWS, BRANCH_W),
                                   lambda i: (jnp.minimum((i + 1) * halo_per_tile, n_halo - 1), col))
    per_layer = lambda a: pl.BlockSpec((None,) + a.shape[1:], lambda i: (layer, 0, 0))
    out = pl.BlockSpec((tl, BRANCH_W), lambda i: (i, 0))
    ext = tl + 2 * HALO_ROWS
    return pl.pallas_call(
        functools.partial(_conv_kernel, tiles_per_seq, tl),
        grid=(t // tl,),
        in_specs=[main(5), main(6), main(7), main(8), main(9),
                  prev(6), prev(7), prev(8), prev(9), nxt(6), nxt(7), nxt(8), nxt(9),
                  per_layer(sc_conv), per_layer(cf_conv), per_layer(cf_ln)],
        out_specs=[out, out],
        out_shape=[jax.ShapeDtypeStruct((t, BRANCH_W), BF16)] * 2,
        scratch_shapes=[pltpu.VMEM((ext, BRANCH_W), F32), pltpu.VMEM((SUBLANES, ext, BRANCH_W), F32)],
        compiler_params=_params("arbitrary"),
        name="conv_branches",
    )(*([parts] * 13), sc_conv, cf_conv, cf_ln)


def _mix_kernel(n_j, y0_ref, y1_ref, y2_ref, y3_ref, g0_ref, g1_ref, g2_ref, g3_ref, wb_ref, wo_ref,
                x_ref, ng_ref, mod_ref, xo_ref, h_ref, acc_ref):
    j = pl.program_id(1)
    merged = None
    for n, (y_ref, g_ref) in enumerate(((y0_ref, g0_ref), (y1_ref, g1_ref), (y2_ref, g2_ref), (y3_ref, g3_ref))):
        p = jnp.dot(y_ref[...].astype(BF16), wb_ref[n], preferred_element_type=F32)
        term = _sigmoid(g_ref[...].astype(F32)) * p
        merged = term if merged is None else merged + term
    contrib = jnp.dot(merged.astype(BF16), wo_ref[...], preferred_element_type=F32)

    @pl.when(j == 0)
    def _():
        acc_ref[...] = contrib

    @pl.when(j != 0)
    def _():
        acc_ref[...] += contrib

    @pl.when(j == n_j - 1)
    def _():
        x1 = x_ref[...] + mod_ref[2:3, :] * (_rms(acc_ref[...]) * ng_ref[1:2, :])
        xo_ref[...] = x1
        h = _rms(x1) * ng_ref[2:3, :]
        h_ref[...] = (h * (1.0 + mod_ref[4:5, :]) + mod_ref[3:4, :]).astype(h_ref.dtype)


def _mix(ys, parts, w_branch, w_out, x, norm_g, mod, layer, mod_span):
    t, d = x.shape
    tm = min(512, mod_span)
    tn = 512
    per_seq = mod_span // tm
    n_j = d // tn
    gate0 = GATE_COL0 // tn
    y_spec = pl.BlockSpec((tm, BRANCH_W), lambda i, j: (i, 0))
    gate = lambda n: pl.BlockSpec((tm, tn), lambda i, j: (i, gate0 + n * n_j + j))
    row_tile = pl.BlockSpec((tm, d), lambda i, j: (i, 0))
    return pl.pallas_call(
        functools.partial(_mix_kernel, n_j),
        grid=(t // tm, n_j),
        in_specs=[y_spec] * 4 + [gate(0), gate(1), gate(2), gate(3),
                                 pl.BlockSpec((None, N_BRANCH, BRANCH_W, tn), lambda i, j: (layer, 0, 0, j)),
                                 pl.BlockSpec((None, tn, d), lambda i, j: (layer, j, 0)),
                                 row_tile,
                                 pl.BlockSpec((None, 4, d), lambda i, j: (layer, 0, 0)),
                                 pl.BlockSpec((None, 6, d), lambda i, j: (i // per_seq, 0, 0))],
        out_specs=[row_tile, row_tile],
        out_shape=[jax.ShapeDtypeStruct((t, d), F32), jax.ShapeDtypeStruct((t, d), BF16)],
        scratch_shapes=[pltpu.VMEM((tm, d), F32)],
        compiler_params=_params("arbitrary", "arbitrary"),
        name="mix",
    )(*ys, parts, parts, parts, parts, w_branch, w_out, x, norm_g, mod)


def _ffn_kernel(n_f, h_ref, w1_ref, w2_ref, x_ref, ng_ref, mod_ref, o_ref, acc_ref):
    f = pl.program_id(1)
    a = jnp.maximum(jnp.dot(h_ref[...], w1_ref[...], preferred_element_type=F32), 0.0)
    contrib = jnp.dot((a * a).astype(BF16), w2_ref[...], preferred_element_type=F32)

    @pl.when(f == 0)
    def _():
        acc_ref[...] = contrib

    @pl.when(f != 0)
    def _():
        acc_ref[...] += contrib

    @pl.when(f == n_f - 1)
    def _():
        o_ref[...] = x_ref[...] + mod_ref[5:6, :] * (_rms(acc_ref[...]) * ng_ref[3:4, :])


def _ffn(h, w1, w2, x, norm_g, mod, layer, mod_span):
    t, d = x.shape
    d_ff = w1.shape[2]
    tm = min(512, mod_span)
    tf = 1024
    per_seq = mod_span // tm
    n_f = d_ff // tf
    row_tile = pl.BlockSpec((tm, d), lambda i, f: (i, 0))
    return pl.pallas_call(
        functools.partial(_ffn_kernel, n_f),
        grid=(t // tm, n_f),
        in_specs=[
            row_tile,
            pl.BlockSpec((None, d, tf), lambda i, f: (layer, 0, f)),
            pl.BlockSpec((None, tf, d), lambda i, f: (layer, f, 0)),
            row_tile,
            pl.BlockSpec((None, 4, d), lambda i, f: (layer, 0, 0)),
            pl.BlockSpec((None, 6, d), lambda i, f: (i // per_seq, 0, 0)),
        ],
        out_specs=row_tile,
        out_shape=jax.ShapeDtypeStruct((t, d), F32),
        scratch_shapes=[pltpu.VMEM((tm, d), F32)],
        compiler_params=_params("arbitrary", "arbitrary"),
        name="ffn",
    )(h, w1, w2, x, norm_g, mod)


def _rope_tables(seq_len):
    rows = seq_len // GRID_W
    row = jnp.repeat(jnp.arange(rows, dtype=F32), GRID_W)
    col = jnp.tile(jnp.arange(GRID_W, dtype=F32), rows)
    freqs = ROPE_BASE ** (-jnp.arange(ROPE_PAIRS, dtype=F32) / ROPE_PAIRS)
    ang = jnp.concatenate([row[:, None] * freqs[None], col[:, None] * freqs[None]], axis=-1)
    cos, sin = jnp.cos(ang), jnp.sin(ang)
    return jnp.concatenate([cos, cos], axis=-1), jnp.concatenate([-sin, sin], axis=-1)


def kernel(x, c, ctx, c_ctx, w_ada, b_ada, norm_g, w_in, ret_decay, sc_conv, cf_conv, cf_ln,
           w_branch, w_out, w_ff1, w_ff2):
    batch, seq_len, d = x.shape
    ctx_len = ctx.shape[1]
    depth = w_ada.shape[0]
    in_cols = w_in.shape[2]
    assert d == D_MODEL and seq_len % (GRID_W * 16) == 0 and ctx_len % RET_CHUNK == 0

    n_rows = -(-(batch + 1) // SUBLANES) * SUBLANES
    c_rows = jnp.concatenate([c, c_ctx[None], jnp.zeros((n_rows - batch - 1, d), F32)], axis=0)
    mod_all = _ada(c_rows, w_ada, b_ada).reshape(depth, n_rows, 6, d)

    rope_tabs = _rope_tables(seq_len)
    xf = x.reshape(batch * seq_len, d)
    xc = ctx.reshape(batch * ctx_len, d)
    zeros_state = jnp.zeros((batch, RET_HEADS, RET_DK, RET_DK), F32)
    w_in_b, wb_b, wo_b, w1_b, w2_b = (w.astype(BF16) for w in (w_in, w_branch, w_out, w_ff1, w_ff2))

    def mixer_tail(parts, y_ret, x_res, mod, layer, length, mod_span):
        y_f = _fourier_mix(parts, batch, length)
        y_sc, y_cf = _conv_branches(parts, sc_conv, cf_conv, cf_ln, layer, length)
        x1, h2 = _mix((y_ret, y_f, y_sc, y_cf), parts, wb_b, wo_b, x_res, norm_g, mod, layer, mod_span)
        return _ffn(h2, w1_b, w2_b, x1, norm_g, mod, layer, mod_span)

    for layer in range(depth):
        with_ctx = layer < depth - 1
        mod = mod_all[layer, :batch]
        mod_c = mod_all[layer, batch:batch + 1]
        tables = _ret_tables(ret_decay[layer].astype(F32))

        ctx_cols = in_cols if with_ctx else 2 * BRANCH_W
        parts_c = _inproj(xc, norm_g, mod_c, w_in_b, layer, ctx_cols, batch * ctx_len)
        stc_f, stc_b, s0f, s0b = _ret_state(parts_c, tables, zeros_state, zeros_state, None, batch, ctx_len)

        parts = _inproj(xf, norm_g, mod, w_in_b, layer, in_cols, seq_len)
        st_f, st_b, _, _ = _ret_state(parts, tables, s0f, s0b, rope_tabs, batch, seq_len)
        y_ret = _ret_out(parts, tables, st_f, st_b, rope_tabs, batch, seq_len)
        xf = mixer_tail(parts, y_ret, xf, mod, layer, seq_len, seq_len)

        if with_ctx:
            y_ret_c = _ret_out(parts_c, tables, stc_f, stc_b, None, batch, ctx_len)
            xc = mixer_tail(parts_c, y_ret_c, xc, mod_c, layer, ctx_len, batch * ctx_len)
    return xf.reshape(batch, seq_len, d)
```

```python
import functools
import math

import numpy as np
import jax
import jax.numpy as jnp
from jax import lax
from jax.experimental import pallas as pl
from jax.experimental.pallas import tpu as pltpu

F32 = jnp.float32
BF16 = jnp.bfloat16

D_MODEL = 2048
GRID_W = 64
N_BRANCH = 4
BRANCH_W = D_MODEL // 4
RET_HEADS = 4
RET_DK = BRANCH_W // RET_HEADS
RET_CHUNK = 128
ROPE_PAIRS = RET_DK // 4
ROPE_BASE = 10000.0
FNET_GROUPS = 4
FNET_GW = BRANCH_W // FNET_GROUPS
SC_WIDTH = 3
CF_WIDTH = 31
GATE_COL0 = 10 * BRANCH_W
EPS = 1e-6
K_SCALE = RET_DK ** -0.5

VMEM_LIMIT_BYTES = 56 * 1024 * 1024
SUBLANES = 8
HALO_ROWS = 16
LANES = 128


def _params(*sem):
    return pltpu.CompilerParams(dimension_semantics=sem, vmem_limit_bytes=VMEM_LIMIT_BYTES)


def _sigmoid(v):
    return 0.5 * jnp.tanh(0.5 * v) + 0.5


def _log_sigmoid(v):
    return jnp.minimum(v, 0.0) - jnp.log1p(jnp.exp(-jnp.abs(v)))


def _rms(v):
    return v * lax.rsqrt(jnp.mean(v * v, axis=-1, keepdims=True) + EPS)


def _ada_kernel(c_ref, w_ref, b_ref, o_ref):
    c = c_ref[...]
    s = c * _sigmoid(c)
    o_ref[...] = jnp.dot(s, w_ref[...], preferred_element_type=F32,
                         precision=lax.Precision.HIGHEST) + b_ref[...]


def _ada(c_rows, w_ada, b_ada):
    depth, d, n = w_ada.shape
    rows = c_rows.shape[0]
    tn = 1024
    return pl.pallas_call(
        _ada_kernel,
        grid=(depth, n // tn),
        in_specs=[
            pl.BlockSpec((rows, d), lambda l, j: (0, 0)),
            pl.BlockSpec((None, d, tn), lambda l, j: (l, 0, j)),
            pl.BlockSpec((None, 1, tn), lambda l, j: (l, 0, j)),
        ],
        out_specs=pl.BlockSpec((None, rows, tn), lambda l, j: (l, 0, j)),
        out_shape=jax.ShapeDtypeStruct((depth, rows, n), F32),
        compiler_params=_params("arbitrary", "arbitrary"),
        name="ada",
    )(c_rows, w_ada, b_ada.reshape(depth, 1, n))


def _inproj_kernel(gate_tile0, x_ref, g_ref, mod_ref, w_ref, o_ref, h_ref):
    j = pl.program_id(1)

    @pl.when(j == 0)
    def _():
        y = _rms(x_ref[...]) * g_ref[0:1, :]
        h = y * (1.0 + mod_ref[1:2, :]) + mod_ref[0:1, :]
        h_ref[...] = h.astype(BF16)

    @pl.when(j < gate_tile0)
    def _():
        o_ref[...] = jnp.dot(h_ref[...], w_ref[...], preferred_element_type=F32).astype(o_ref.dtype)

    @pl.when(j >= gate_tile0)
    def _():
        o_ref[...] = _sigmoid(jnp.dot(h_ref[...], w_ref[...], preferred_element_type=F32)).astype(o_ref.dtype)


def _inproj(x, norm_g, mod, w, layer, n_cols, mod_span):
    t, d = x.shape
    tm = min(1024, mod_span)
    tn = min(1024, n_cols)
    assert GATE_COL0 % tn == 0 or n_cols <= GATE_COL0
    per_seq = mod_span // tm
    return pl.pallas_call(
        functools.partial(_inproj_kernel, GATE_COL0 // tn),
        grid=(t // tm, n_cols // tn),
        in_specs=[
            pl.BlockSpec((tm, d), lambda i, j: (i, 0)),
            pl.BlockSpec((None, 4, d), lambda i, j: (layer, 0, 0)),
            pl.BlockSpec((None, 6, d), lambda i, j: (i // per_seq, 0, 0)),
            pl.BlockSpec((None, d, tn), lambda i, j: (layer, 0, j)),
        ],
        out_specs=pl.BlockSpec((tm, tn), lambda i, j: (i, j)),
        out_shape=jax.ShapeDtypeStruct((t, n_cols), BF16),
        scratch_shapes=[pltpu.VMEM((tm, d), BF16)],
        compiler_params=_params("arbitrary", "arbitrary"),
        name="inproj",
    )(x, norm_g, mod, w)


def _ret_tables_kernel(dec_ref, dmat_ref, qdec_ref, kdec_ref, cdec_ref):
    row = lax.broadcasted_iota(jnp.int32, (RET_CHUNK, RET_CHUNK), 0).astype(F32)
    col = lax.broadcasted_iota(jnp.int32, (RET_CHUNK, RET_CHUNK), 1).astype(F32)
    for h in range(RET_HEADS):
        lg_f = _log_sigmoid(jnp.full((1, RET_DK), dec_ref[0, h], F32))
        lg_b = _log_sigmoid(jnp.full((1, RET_DK), dec_ref[1, h], F32))
        dmat_ref[h] = K_SCALE * jnp.where(row >= col, jnp.exp(lg_f * jnp.maximum(row - col, 0.0)),
                                          jnp.exp(lg_b * jnp.maximum(col - row, 0.0)))
        qdec_ref[0, h] = jnp.exp(lg_f * (row + 1.0))
        qdec_ref[1, h] = jnp.exp(lg_b * (RET_CHUNK - row))
        kdec_ref[0, h] = K_SCALE * jnp.exp(lg_f * (RET_CHUNK - 1.0 - row))
        kdec_ref[1, h] = K_SCALE * jnp.exp(lg_b * row)
        cdec_ref[0, h] = jnp.broadcast_to(jnp.exp(lg_f * RET_CHUNK), (SUBLANES, RET_DK))
        cdec_ref[1, h] = jnp.broadcast_to(jnp.exp(lg_b * RET_CHUNK), (SUBLANES, RET_DK))


def _ret_tables(dec):
    sq = (RET_HEADS, RET_CHUNK, RET_CHUNK)
    return pl.pallas_call(
        _ret_tables_kernel,
        in_specs=[pl.BlockSpec(memory_space=pltpu.SMEM)],
        out_shape=[jax.ShapeDtypeStruct(sq, F32), jax.ShapeDtypeStruct((2,) + sq, F32),
                   jax.ShapeDtypeStruct((2,) + sq, F32),
                   jax.ShapeDtypeStruct((2, RET_HEADS, SUBLANES, RET_DK), F32)],
        name="ret_tables",
    )(dec)


def _rope(t, cos, sin):
    return t * cos + pltpu.roll(t, RET_DK // 2, 1) * sin


def _ret_state_kernel(use_rope, cps, n_steps, *refs):
    if use_rope:
        (kdec_ref, cdec_ref, kf_ref, vf_ref, kb_ref, vb_ref, cf_ref, sf_ref, cb_ref, sb_ref, s0f_ref, s0b_ref,
         of_ref, ob_ref, ff_ref, fb_ref, st_ref) = refs
    else:
        (kdec_ref, cdec_ref, kf_ref, vf_ref, kb_ref, vb_ref, s0f_ref, s0b_ref,
         of_ref, ob_ref, ff_ref, fb_ref, st_ref) = refs
    i = pl.program_id(1)

    @pl.when(i == 0)
    def _():
        st_ref[0] = s0f_ref[...]
        st_ref[1] = s0b_ref[...]

    for cc in range(cps):
        rf = slice(cc * RET_CHUNK, (cc + 1) * RET_CHUNK)
        rb = slice((cps - 1 - cc) * RET_CHUNK, (cps - cc) * RET_CHUNK)
        for h in range(RET_HEADS):
            hs = slice(h * RET_DK, (h + 1) * RET_DK)
            kf = kf_ref[rf, hs].astype(F32)
            kb = kb_ref[rb, hs].astype(F32)
            if use_rope:
                kf = _rope(kf, cf_ref[rf, :], sf_ref[rf, :])
                kb = _rope(kb, cb_ref[rb, :], sb_ref[rb, :])
            kf = kf * kdec_ref[0, h]
            kb = kb * kdec_ref[1, h]
            upd_f = jnp.dot(kf.T.astype(BF16), vf_ref[rf, hs], preferred_element_type=F32)
            upd_b = jnp.dot(kb.T.astype(BF16), vb_ref[rb, hs], preferred_element_type=F32)
            s_f = st_ref[0, h]
            s_b = st_ref[1, h]
            of_ref[cc, h] = s_f.astype(of_ref.dtype)
            ob_ref[cps - 1 - cc, h] = s_b.astype(ob_ref.dtype)
            st_ref[0, h] = s_f * cdec_ref[0, h, 0:1, :] + upd_f
            st_ref[1, h] = s_b * cdec_ref[1, h, 0:1, :] + upd_b

    @pl.when(i == n_steps - 1)
    def _():
        ff_ref[...] = st_ref[0]
        fb_ref[...] = st_ref[1]


def _chunks_per_step(n_chunks):
    return math.gcd(n_chunks, 4)


def _ret_state(parts, tables, s0f, s0b, rope_tabs, batch, seq_len):
    n = seq_len // RET_CHUNK
    cps = _chunks_per_step(n)
    ns = n // cps
    rows = cps * RET_CHUNK
    use_rope = rope_tabs is not None
    _, _, kdec, cdec = tables
    blk = lambda col, rev: pl.BlockSpec(
        (rows, BRANCH_W), (lambda b, i: (b * ns + ns - 1 - i, col)) if rev else (lambda b, i: (b * ns + i, col)))
    tab = lambda rev: pl.BlockSpec((rows, RET_DK), (lambda b, i: (ns - 1 - i, 0)) if rev else (lambda b, i: (i, 0)))
    st_in = pl.BlockSpec((None, RET_HEADS, RET_DK, RET_DK), lambda b, i: (b, 0, 0, 0))
    full = lambda a: pl.BlockSpec(a.shape, lambda b, i: (0,) * a.ndim)
    in_specs = [full(kdec), full(cdec), blk(0, False), blk(1, False), blk(0, True), blk(1, True)]
    args = [kdec, cdec, parts, parts, parts, parts]
    if use_rope:
        in_specs += [tab(False), tab(False), tab(True), tab(True)]
        args += [rope_tabs[0], rope_tabs[1], rope_tabs[0], rope_tabs[1]]
    in_specs += [st_in, st_in]
    args += [s0f, s0b]
    st_shape = jax.ShapeDtypeStruct((batch, n, RET_HEADS, RET_DK, RET_DK), BF16)
    fin_shape = jax.ShapeDtypeStruct((batch, RET_HEADS, RET_DK, RET_DK), F32)
    return pl.pallas_call(
        functools.partial(_ret_state_kernel, use_rope, cps, ns),
        grid=(batch, ns),
        in_specs=in_specs,
        out_specs=[
            pl.BlockSpec((None, cps, RET_HEADS, RET_DK, RET_DK), lambda b, i: (b, i, 0, 0, 0)),
            pl.BlockSpec((None, cps, RET_HEADS, RET_DK, RET_DK), lambda b, i: (b, ns - 1 - i, 0, 0, 0)),
            st_in, st_in,
        ],
        out_shape=[st_shape, st_shape, fin_shape, fin_shape],
        scratch_shapes=[pltpu.VMEM((2, RET_HEADS, RET_DK, RET_DK), F32)],
        compiler_params=_params("arbitrary", "arbitrary"),
        name="ret_state",
    )(*args)


def _ret_out_kernel(use_rope, cps, *refs):
    if use_rope:
        dmat_ref, qdec_ref, k_ref, v_ref, q_ref, g_ref, cos_ref, sin_ref, sf_ref, sb_ref, o_ref = refs
    else:
        dmat_ref, qdec_ref, k_ref, v_ref, q_ref, g_ref, sf_ref, sb_ref, o_ref = refs
    for cc in range(cps):
        rs = slice(cc * RET_CHUNK, (cc + 1) * RET_CHUNK)
        for h in range(RET_HEADS):
            hs = slice(h * RET_DK, (h + 1) * RET_DK)
            q = q_ref[rs, hs].astype(F32)
            k = k_ref[rs, hs].astype(F32)
            if use_rope:
                cos, sin = cos_ref[rs, :], sin_ref[rs, :]
                q = _rope(q, cos, sin)
                k = _rope(k, cos, sin)
            scores = lax.dot_general(q.astype(BF16), k.astype(BF16), (((1,), (1,)), ((), ())),
                                     preferred_element_type=F32)
            y = jnp.dot((scores * dmat_ref[h]).astype(BF16), v_ref[rs, hs], preferred_element_type=F32)
            y = y + jnp.dot((q * qdec_ref[0, h]).astype(BF16), sf_ref[cc, h], preferred_element_type=F32)
            y = y + jnp.dot((q * qdec_ref[1, h]).astype(BF16), sb_ref[cc, h], preferred_element_type=F32)
            mu = jnp.mean(y, axis=-1, keepdims=True)
            yc = y - mu
            yn = yc * lax.rsqrt(jnp.mean(yc * yc, axis=-1, keepdims=True) + EPS)
            g = g_ref[rs, hs].astype(F32)
            o_ref[rs, hs] = (yn * (g * _sigmoid(g))).astype(o_ref.dtype)


def _ret_out(parts, tables, st_f, st_b, rope_tabs, batch, seq_len):
    n = seq_len // RET_CHUNK
    cps = _chunks_per_step(n)
    ns = n // cps
    rows = cps * RET_CHUNK
    use_rope = rope_tabs is not None
    dmat, qdec, _, _ = tables
    blk = lambda col: pl.BlockSpec((rows, BRANCH_W), lambda b, i: (b * ns + i, col))
    st = pl.BlockSpec((None, cps, RET_HEADS, RET_DK, RET_DK), lambda b, i: (b, i, 0, 0, 0))
    full = lambda a: pl.BlockSpec(a.shape, lambda b, i: (0,) * a.ndim)
    in_specs = [full(dmat), full(qdec), blk(0), blk(1), blk(2), blk(3)]
    args = [dmat, qdec, parts, parts, parts, parts]
    if use_rope:
        tab = pl.BlockSpec((rows, RET_DK), lambda b, i: (i, 0))
        in_specs += [tab, tab]
        args += list(rope_tabs)
    in_specs += [st, st]
    args += [st_f, st_b]
    return pl.pallas_call(
        functools.partial(_ret_out_kernel, use_rope, cps),
        grid=(batch, ns),
        in_specs=in_specs,
        out_specs=blk(0),
        out_shape=jax.ShapeDtypeStruct((batch * seq_len, BRANCH_W), BF16),
        compiler_params=_params("arbitrary", "arbitrary"),
        name="ret_out",
    )(*args)


def _chan_dft_kernel(x_ref, m_ref, o_ref):
    for g in range(FNET_GROUPS):
        r = jnp.dot(x_ref[:, g * FNET_GW:(g + 1) * FNET_GW], m_ref[...], preferred_element_type=F32)
        o_ref[:, g * FNET_GW:(g + 1) * FNET_GW] = r[:, :FNET_GW]
        o_ref[:, BRANCH_W + g * FNET_GW:BRANCH_W + (g + 1) * FNET_GW] = r[:, FNET_GW:]


def _chan_dft(parts, chan_mat):
    t = parts.shape[0]
    tm = min(1024, t)
    return pl.pallas_call(
        _chan_dft_kernel,
        grid=(t // tm,),
        in_specs=[pl.BlockSpec((tm, BRANCH_W), lambda i: (i, 4)),
                  pl.BlockSpec((FNET_GW, 2 * FNET_GW), lambda i: (0, 0))],
        out_specs=pl.BlockSpec((tm, 2 * BRANCH_W), lambda i: (i, 0)),
        out_shape=jax.ShapeDtypeStruct((t, 2 * BRANCH_W), F32),
        compiler_params=_params("arbitrary"),
        name="chan_dft",
    )(parts, chan_mat)


def _fft_rows_kernel(rows, cols_per_step, x_ref, f_ref, tc_ref, ts_ref, o_ref):
    n = rows * cols_per_step
    x2d = x_ref[...].reshape(n, 2 * BRANCH_W).astype(BF16)
    res = jnp.dot(f_ref[...], x2d, preferred_element_type=F32)
    tc = tc_ref[...]
    ts = ts_ref[...]
    for g in range(FNET_GROUPS):
        u0 = g * FNET_GW
        v0 = u0 + BRANCH_W
        a_re = res[:n, u0:u0 + FNET_GW] - res[n:, v0:v0 + FNET_GW]
        a_im = -(res[:n, v0:v0 + FNET_GW] + res[n:, u0:u0 + FNET_GW])
        gs = slice(g * FNET_GW, (g + 1) * FNET_GW)
        o_ref[0, :, :, gs] = (a_re * tc + a_im * ts).reshape(rows, cols_per_step, FNET_GW)
        o_ref[1, :, :, gs] = (a_im * tc - a_re * ts).reshape(rows, cols_per_step, FNET_GW)


def _fft_rows(uv, f_mat, tw_cos, tw_sin, batch, rows):
    cols_per_step = SUBLANES
    n = rows * cols_per_step
    x = uv.reshape(batch, rows, GRID_W, 2 * BRANCH_W)
    return pl.pallas_call(
        functools.partial(_fft_rows_kernel, rows, cols_per_step),
        grid=(batch, GRID_W // cols_per_step),
        in_specs=[
            pl.BlockSpec((None, rows, cols_per_step, 2 * BRANCH_W), lambda b, j: (b, 0, j, 0)),
            pl.BlockSpec((2 * n, n), lambda b, j: (0, 0)),
            pl.BlockSpec((None, n, FNET_GW), lambda b, j: (j, 0, 0)),
            pl.BlockSpec((None, n, FNET_GW), lambda b, j: (j, 0, 0)),
        ],
        out_specs=pl.BlockSpec((None, 2, rows, cols_per_step, BRANCH_W), lambda b, j: (b, 0, 0, j, 0)),
        out_shape=jax.ShapeDtypeStruct((batch, 2, rows, GRID_W, BRANCH_W), F32),
        compiler_params=_params("arbitrary", "arbitrary"),
        name="fft_rows",
    )(x, f_mat, tw_cos, tw_sin)


def _fft_cols_kernel(scale, k1_per_step, p_ref, f_ref, o_ref):
    for k1 in range(k1_per_step):
        stacked = jnp.concatenate([p_ref[0, k1], p_ref[1, k1]], axis=0).astype(BF16)
        o_ref[:, k1, :] = jnp.dot(f_ref[...], stacked, preferred_element_type=F32) * scale


def _fft_cols(p, f_mat, batch, rows, scale):
    k1_per_step = SUBLANES
    out = pl.pallas_call(
        functools.partial(_fft_cols_kernel, scale, k1_per_step),
        grid=(batch, rows // k1_per_step),
        in_specs=[
            pl.BlockSpec((None, 2, k1_per_step, GRID_W, BRANCH_W), lambda b, j: (b, 0, j, 0, 0)),
            pl.BlockSpec((GRID_W, 2 * GRID_W), lambda b, j: (0, 0)),
        ],
        out_specs=pl.BlockSpec((None, GRID_W, k1_per_step, BRANCH_W), lambda b, j: (b, 0, j, 0)),
        out_shape=jax.ShapeDtypeStruct((batch, GRID_W, rows, BRANCH_W), F32),
        compiler_params=_params("arbitrary", "arbitrary"),
        name="fft_cols",
    )(p, f_mat)
    return out.reshape(batch * GRID_W * rows, BRANCH_W)


def _fnet_dense_kernel(scale, uv_ref, c_ref, s_ref, o_ref):
    y = jnp.dot(c_ref[...], uv_ref[:, :BRANCH_W].astype(BF16), preferred_element_type=F32)
    y = y - jnp.dot(s_ref[...], uv_ref[:, BRANCH_W:].astype(BF16), preferred_element_type=F32)
    o_ref[...] = y * scale


def _fnet_dense(uv, cos_mat, sin_mat, batch, seq_len, scale):
    return pl.pallas_call(
        functools.partial(_fnet_dense_kernel, scale),
        grid=(batch,),
        in_specs=[pl.BlockSpec((seq_len, 2 * BRANCH_W), lambda b: (b, 0)),
                  pl.BlockSpec((seq_len, seq_len), lambda b: (0, 0)),
                  pl.BlockSpec((seq_len, seq_len), lambda b: (0, 0))],
        out_specs=pl.BlockSpec((seq_len, BRANCH_W), lambda b: (b, 0)),
        out_shape=jax.ShapeDtypeStruct((batch * seq_len, BRANCH_W), F32),
        compiler_params=_params("arbitrary"),
        name="fnet_dense",
    )(uv, cos_mat, sin_mat)


def _dft_cos_sin(n):
    idx = np.arange(n)
    ang = 2.0 * np.pi * ((np.outer(idx, idx)) % n) / n
    return np.cos(ang), np.sin(ang)


def _mxu_const(a):
    return jnp.asarray(a, F32).astype(BF16)


def _fourier_mix(parts, batch, seq_len):
    scale = 1.0 / math.sqrt(seq_len * FNET_GW)
    rows = seq_len // GRID_W
    c_ch, s_ch = _dft_cos_sin(FNET_GW)
    uv = _chan_dft(parts, _mxu_const(np.concatenate([c_ch, s_ch], axis=1)))
    if rows % 16 == 0:
        c_r, s_r = _dft_cos_sin(rows)
        eye = np.eye(SUBLANES)
        row_mat = _mxu_const(np.concatenate([np.kron(c_r, eye), np.kron(s_r, eye)], axis=0))
        c_c, s_c = _dft_cos_sin(GRID_W)
        col_mat = _mxu_const(np.concatenate([c_c, s_c], axis=1))
        ang = 2.0 * np.pi * np.outer(np.arange(rows), np.arange(GRID_W)) / seq_len
        ang = ang.reshape(rows, GRID_W // SUBLANES, SUBLANES).transpose(1, 0, 2).reshape(-1, rows * SUBLANES)
        tw_cos = jnp.asarray(np.broadcast_to(np.cos(ang)[:, :, None], ang.shape + (FNET_GW,)), F32)
        tw_sin = jnp.asarray(np.broadcast_to(np.sin(ang)[:, :, None], ang.shape + (FNET_GW,)), F32)
        p = _fft_rows(uv, row_mat, tw_cos, tw_sin, batch, rows)
        return _fft_cols(p, col_mat, batch, rows, scale)
    c_l, s_l = _dft_cos_sin(seq_len)
    return _fnet_dense(uv, _mxu_const(c_l), _mxu_const(s_l), batch, seq_len, scale)


def _conv_kernel(tiles_per_seq, tl, scb_ref, scc_ref, scx_ref, cfa_ref, cfb_ref,
                 scc_p, scx_p, cfa_p, cfb_p, scc_n, scx_n, cfa_n, cfb_n,
                 wsc_ref, wcf_ref, ln_ref, osc_ref, ocf_ref, esc_ref, ecf_ref):
    i = pl.program_id(0)
    keep_prev = jnp.where(i % tiles_per_seq == 0, 0.0, 1.0)
    keep_next = jnp.where(i % tiles_per_seq == tiles_per_seq - 1, 0.0, 1.0)
    ext = tl + 2 * HALO_ROWS

    def gated(a_ref, b_ref):
        return a_ref[...].astype(F32) * _sigmoid(b_ref[...].astype(F32))

    def prod(a_ref, b_ref):
        return a_ref[...].astype(F32) * b_ref[...].astype(F32)

    esc_ref[0:HALO_ROWS] = prod(scc_p, scx_p) * keep_prev
    esc_ref[HALO_ROWS:HALO_ROWS + tl] = prod(scc_ref, scx_ref)
    esc_ref[HALO_ROWS + tl:] = prod(scc_n, scx_n) * keep_next
    ecf_ref[0, 0:HALO_ROWS] = gated(cfa_p, cfb_p) * keep_prev
    ecf_ref[0, HALO_ROWS:HALO_ROWS + tl] = gated(cfa_ref, cfb_ref)
    ecf_ref[0, HALO_ROWS + tl:ext] = gated(cfa_n, cfb_n) * keep_next
    for s in range(1, SUBLANES):
        ecf_ref[s, 0:ext - SUBLANES] = ecf_ref[0, s:ext - SUBLANES + s]

    rs = 32
    lane_groups = BRANCH_W // LANES
    for r in range(tl // rs):
        r0 = r * rs
        accs = []
        for lg in range(lane_groups):
            ls = slice(lg * LANES, (lg + 1) * LANES)
            acc = jnp.zeros((rs, LANES), F32)
            for j in range(SC_WIDTH):
                off = HALO_ROWS + r0 + j - (SC_WIDTH - 1) // 2
                acc = acc + wsc_ref[j:j + 1, ls] * esc_ref[off:off + rs, ls]
            osc_ref[r0:r0 + rs, ls] = (scb_ref[r0:r0 + rs, ls].astype(F32) * acc).astype(osc_ref.dtype)
            acc = jnp.zeros((rs, LANES), F32)
            for j in range(CF_WIDTH):
                off = HALO_ROWS + r0 + j - (CF_WIDTH - 1) // 2
                s = off % SUBLANES
                acc = acc + wcf_ref[j:j + 1, ls] * ecf_ref[s, off - s:off - s + rs, ls]
            accs.append(acc)
        tot = accs[0]
        for a in accs[1:]:
            tot = tot + a
        mu = jnp.sum(tot, axis=-1, keepdims=True) * (1.0 / BRANCH_W)
        sq = None
        for a in accs:
            d = a - mu
            sq = d * d if sq is None else sq + d * d
        inv = lax.rsqrt(jnp.sum(sq, axis=-1, keepdims=True) * (1.0 / BRANCH_W) + EPS)
        for lg, a in enumerate(accs):
            ls = slice(lg * LANES, (lg + 1) * LANES)
            z = (a - mu) * inv * ln_ref[0:1, ls] + ln_ref[1:2, ls]
            ocf_ref[r0:r0 + rs, ls] = (z * _sigmoid(z)).astype(ocf_ref.dtype)


def _conv_branches(parts, sc_conv, cf_conv, cf_ln, layer, seq_len):
    t = parts.shape[0]
    tl = min(256, seq_len)
    assert seq_len % tl == 0 and tl % HALO_ROWS == 0
    tiles_per_seq = seq_len // tl
    halo_per_tile = tl // HALO_ROWS
    n_halo = t // HALO_ROWS
    main = lambda col: pl.BlockSpec((tl, BRANCH_W), lambda i: (i, col))
    prev = lambda col: pl.BlockSpec((HALO_ROWS, BRANCH_W), lambda i: (jnp.maximum(i * halo_per_tile - 1, 0), col))
    nxt = lambda col: pl.BlockSpec((HALO_ROWS, BRANCH_W),
                                   lambda i: (jnp.minimum((i + 1) * halo_per_tile, n_halo - 1), col))
    per_layer = lambda a: pl.BlockSpec((None,) + a.shape[1:], lambda i: (layer, 0, 0))
    out = pl.BlockSpec((tl, BRANCH_W), lambda i: (i, 0))
    ext = tl + 2 * HALO_ROWS
    return pl.pallas_call(
        functools.partial(_conv_kernel, tiles_per_seq, tl),
        grid=(t // tl,),
        in_specs=[main(5), main(6), main(7), main(8), main(9),
                  prev(6), prev(7), prev(8), prev(9), nxt(6), nxt(7), nxt(8), nxt(9),
                  per_layer(sc_conv), per_layer(cf_conv), per_layer(cf_ln)],
        out_specs=[out, out],
        out_shape=[jax.ShapeDtypeStruct((t, BRANCH_W), BF16)] * 2,
        scratch_shapes=[pltpu.VMEM((ext, BRANCH_W), F32), pltpu.VMEM((SUBLANES, ext, BRANCH_W), F32)],
        compiler_params=_params("arbitrary"),
        name="conv_branches",
    )(*([parts] * 13), sc_conv, cf_conv, cf_ln)


def _mix_kernel(n_j, y0_ref, y1_ref, y2_ref, y3_ref, g0_ref, g1_ref, g2_ref, g3_ref, wb_ref, wo_ref,
                x_ref, ng_ref, mod_ref, xo_ref, h_ref, acc_ref):
    j = pl.program_id(1)

    @pl.when(j == 0)
    def _():
        acc_ref[...] = jnp.zeros_like(acc_ref)

    merged = None
    for n, (y_ref, g_ref) in enumerate(((y0_ref, g0_ref), (y1_ref, g1_ref), (y2_ref, g2_ref), (y3_ref, g3_ref))):
        p = jnp.dot(y_ref[...].astype(BF16), wb_ref[n], preferred_element_type=F32)
        term = g_ref[...].astype(F32) * p
        merged = term if merged is None else merged + term
    acc_ref[...] += jnp.dot(merged.astype(BF16), wo_ref[...], preferred_element_type=F32)

    @pl.when(j == n_j - 1)
    def _():
        x1 = x_ref[...] + mod_ref[2:3, :] * (_rms(acc_ref[...]) * ng_ref[1:2, :])
        xo_ref[...] = x1
        h = _rms(x1) * ng_ref[2:3, :]
        h_ref[...] = (h * (1.0 + mod_ref[4:5, :]) + mod_ref[3:4, :]).astype(h_ref.dtype)


def _mix(ys, parts, w_branch, w_out, x, norm_g, mod, layer, mod_span):
    t, d = x.shape
    tm = min(512, mod_span)
    tn = 512
    per_seq = mod_span // tm
    n_j = d // tn
    gate0 = GATE_COL0 // tn
    y_spec = pl.BlockSpec((tm, BRANCH_W), lambda i, j: (i, 0))
    gate = lambda n: pl.BlockSpec((tm, tn), lambda i, j: (i, gate0 + n * n_j + j))
    row_tile = pl.BlockSpec((tm, d), lambda i, j: (i, 0))
    return pl.pallas_call(
        functools.partial(_mix_kernel, n_j),
        grid=(t // tm, n_j),
        in_specs=[y_spec] * 4 + [gate(0), gate(1), gate(2), gate(3),
                                 pl.BlockSpec((None, N_BRANCH, BRANCH_W, tn), lambda i, j: (layer, 0, 0, j)),
                                 pl.BlockSpec((None, tn, d), lambda i, j: (layer, j, 0)),
                                 row_tile,
                                 pl.BlockSpec((None, 4, d), lambda i, j: (layer, 0, 0)),
                                 pl.BlockSpec((None, 6, d), lambda i, j: (i // per_seq, 0, 0))],
        out_specs=[row_tile, row_tile],
        out_shape=[jax.ShapeDtypeStruct((t, d), F32), jax.ShapeDtypeStruct((t, d), BF16)],
        scratch_shapes=[pltpu.VMEM((tm, d), F32)],
        compiler_params=_params("arbitrary", "arbitrary"),
        name="mix",
    )(*ys, parts, parts, parts, parts, w_branch, w_out, x, norm_g, mod)


def _ffn_kernel(n_f, h_ref, w1_ref, w2_ref, x_ref, ng_ref, mod_ref, o_ref, acc_ref):
    f = pl.program_id(1)

    @pl.when(f == 0)
    def _():
        acc_ref[...] = jnp.zeros_like(acc_ref)

    a = jnp.maximum(jnp.dot(h_ref[...], w1_ref[...], preferred_element_type=F32), 0.0)
    acc_ref[...] += jnp.dot((a * a).astype(BF16), w2_ref[...], preferred_element_type=F32)

    @pl.when(f == n_f - 1)
    def _():
        o_ref[...] = x_ref[...] + mod_ref[5:6, :] * (_rms(acc_ref[...]) * ng_ref[3:4, :])


def _ffn(h, w1, w2, x, norm_g, mod, layer, mod_span):
    t, d = x.shape
    d_ff = w1.shape[2]
    tm = min(512, mod_span)
    tf = 1024
    per_seq = mod_span // tm
    n_f = d_ff // tf
    row_tile = pl.BlockSpec((tm, d), lambda i, f: (i, 0))
    return pl.pallas_call(
        functools.partial(_ffn_kernel, n_f),
        grid=(t // tm, n_f),
        in_specs=[
            row_tile,
            pl.BlockSpec((None, d, tf), lambda i, f: (layer, 0, f)),
            pl.BlockSpec((None, tf, d), lambda i, f: (layer, f, 0)),
            row_tile,
            pl.BlockSpec((None, 4, d), lambda i, f: (layer, 0, 0)),
            pl.BlockSpec((None, 6, d), lambda i, f: (i // per_seq, 0, 0)),
        ],
        out_specs=row_tile,
        out_shape=jax.ShapeDtypeStruct((t, d), F32),
        scratch_shapes=[pltpu.VMEM((tm, d), F32)],
        compiler_params=_params("arbitrary", "arbitrary"),
        name="ffn",
    )(h, w1, w2, x, norm_g, mod)


def _rope_tables(seq_len):
    rows = seq_len // GRID_W
    row = jnp.repeat(jnp.arange(rows, dtype=F32), GRID_W)
    col = jnp.tile(jnp.arange(GRID_W, dtype=F32), rows)
    freqs = ROPE_BASE ** (-jnp.arange(ROPE_PAIRS, dtype=F32) / ROPE_PAIRS)
    ang = jnp.concatenate([row[:, None] * freqs[None], col[:, None] * freqs[None]], axis=-1)
    cos, sin = jnp.cos(ang), jnp.sin(ang)
    return jnp.concatenate([cos, cos], axis=-1), jnp.concatenate([-sin, sin], axis=-1)


def kernel(x, c, ctx, c_ctx, w_ada, b_ada, norm_g, w_in, ret_decay, sc_conv, cf_conv, cf_ln,
           w_branch, w_out, w_ff1, w_ff2):
    batch, seq_len, d = x.shape
    ctx_len = ctx.shape[1]
    depth = w_ada.shape[0]
    in_cols = w_in.shape[2]
    assert d == D_MODEL and seq_len % (GRID_W * 16) == 0 and ctx_len % RET_CHUNK == 0

    n_rows = -(-(batch + 1) // SUBLANES) * SUBLANES
    c_rows = jnp.concatenate([c, c_ctx[None], jnp.zeros((n_rows - batch - 1, d), F32)], axis=0)
    mod_all = _ada(c_rows, w_ada, b_ada).reshape(depth, n_rows, 6, d)

    rope_tabs = _rope_tables(seq_len)
    xf = x.reshape(batch * seq_len, d)
    xc = ctx.reshape(batch * ctx_len, d)
    zeros_state = jnp.zeros((batch, RET_HEADS, RET_DK, RET_DK), F32)
    w_in_b, wb_b, wo_b, w1_b, w2_b = (w.astype(BF16) for w in (w_in, w_branch, w_out, w_ff1, w_ff2))

    def mixer_tail(parts, y_ret, x_res, mod, layer, length, mod_span):
        y_f = _fourier_mix(parts, batch, length)
        y_sc, y_cf = _conv_branches(parts, sc_conv, cf_conv, cf_ln, layer, length)
        x1, h2 = _mix((y_ret, y_f, y_sc, y_cf), parts, wb_b, wo_b, x_res, norm_g, mod, layer, mod_span)
        return _ffn(h2, w1_b, w2_b, x1, norm_g, mod, layer, mod_span)

    for layer in range(depth):
        with_ctx = layer < depth - 1
        mod = mod_all[layer, :batch]
        mod_c = mod_all[layer, batch:batch + 1]
        tables = _ret_tables(ret_decay[layer].astype(F32))

        ctx_cols = in_cols if with_ctx else 2 * BRANCH_W
        parts_c = _inproj(xc, norm_g, mod_c, w_in_b, layer, ctx_cols, batch * ctx_len)
        stc_f, stc_b, s0f, s0b = _ret_state(parts_c, tables, zeros_state, zeros_state, None, batch, ctx_len)

        parts = _inproj(xf, norm_g, mod, w_in_b, layer, in_cols, seq_len)
        st_f, st_b, _, _ = _ret_state(parts, tables, s0f, s0b, rope_tabs, batch, seq_len)
        y_ret = _ret_out(parts, tables, st_f, st_b, rope_tabs, batch, seq_len)
        xf = mixer_tail(parts, y_ret, xf, mod, layer, seq_len, seq_len)

        if with_ctx:
            y_ret_c = _ret_out(parts_c, tables, stc_f, stc_b, None, batch, ctx_len)
            xc = mixer_tail(parts_c, y_ret_c, xc, mod_c, layer, ctx_len, batch * ctx_len)
    return xf.reshape(batch, seq_len, d)
```

```python
import functools
import math

import numpy as np
import jax
import jax.numpy as jnp
from jax import lax
from jax.experimental import pallas as pl
from jax.experimental.pallas import tpu as pltpu

F32 = jnp.float32
BF16 = jnp.bfloat16

D_MODEL = 2048
GRID_W = 64
N_BRANCH = 4
BRANCH_W = D_MODEL // 4
RET_HEADS = 4
RET_DK = BRANCH_W // RET_HEADS
RET_CHUNK = 128
ROPE_PAIRS = RET_DK // 4
ROPE_BASE = 10000.0
FNET_GROUPS = 4
FNET_GW = BRANCH_W // FNET_GROUPS
SC_WIDTH = 3
CF_WIDTH = 31
GATE_COL0 = 10 * BRANCH_W
EPS = 1e-6
K_SCALE = RET_DK ** -0.5

VMEM_LIMIT_BYTES = 56 * 1024 * 1024
SUBLANES = 8
BF16_ROWS = 16
HALO_ROWS = BF16_ROWS
LANES = 128


def _params(*sem):
    return pltpu.CompilerParams(dimension_semantics=sem, vmem_limit_bytes=VMEM_LIMIT_BYTES)


def _sigmoid(v):
    return 0.5 * jnp.tanh(0.5 * v) + 0.5


def _log_sigmoid(v):
    return jnp.minimum(v, 0.0) - jnp.log1p(jnp.exp(-jnp.abs(v)))


def _rms(v):
    return v * lax.rsqrt(jnp.mean(v * v, axis=-1, keepdims=True) + EPS)


def _ada_kernel(c_ref, w_ref, b_ref, o_ref):
    c = c_ref[...]
    s = c * _sigmoid(c)
    w = w_ref[...]
    s_hi = s.astype(BF16)
    s_lo = (s - s_hi.astype(F32)).astype(BF16)
    w_hi = w.astype(BF16)
    w_lo = (w - w_hi.astype(F32)).astype(BF16)
    acc = jnp.dot(s_hi, w_hi, preferred_element_type=F32)
    acc = acc + jnp.dot(s_hi, w_lo, preferred_element_type=F32)
    acc = acc + jnp.dot(s_lo, w_hi, preferred_element_type=F32)
    o_ref[...] = acc + b_ref[...]


def _ada(c_rows, w_ada, b_ada):
    depth, d, n = w_ada.shape
    rows = c_rows.shape[0]
    tn = 1024
    return pl.pallas_call(
        _ada_kernel,
        grid=(depth, n // tn),
        in_specs=[
            pl.BlockSpec((rows, d), lambda l, j: (0, 0)),
            pl.BlockSpec((None, d, tn), lambda l, j: (l, 0, j)),
            pl.BlockSpec((None, 1, tn), lambda l, j: (l, 0, j)),
        ],
        out_specs=pl.BlockSpec((None, rows, tn), lambda l, j: (l, 0, j)),
        out_shape=jax.ShapeDtypeStruct((depth, rows, n), F32),
        compiler_params=_params("arbitrary", "arbitrary"),
        name="ada",
    )(c_rows, w_ada, b_ada.reshape(depth, 1, n))


def _inproj_kernel(gate_tile0, x_ref, g_ref, mod_ref, w_ref, o_ref, h_ref):
    j = pl.program_id(1)

    @pl.when(j == 0)
    def _():
        y = _rms(x_ref[...]) * g_ref[0:1, :]
        h = y * (1.0 + mod_ref[1:2, :]) + mod_ref[0:1, :]
        h_ref[...] = h.astype(BF16)

    @pl.when(j < gate_tile0)
    def _():
        o_ref[...] = jnp.dot(h_ref[...], w_ref[...], preferred_element_type=F32).astype(o_ref.dtype)

    @pl.when(j >= gate_tile0)
    def _():
        o_ref[...] = _sigmoid(jnp.dot(h_ref[...], w_ref[...], preferred_element_type=F32)).astype(o_ref.dtype)


def _inproj(x, norm_g, mod, w, layer, n_cols, mod_span):
    t, d = x.shape
    tm = min(1024, mod_span)
    tn = min(1024, n_cols)
    assert GATE_COL0 % tn == 0 or n_cols <= GATE_COL0
    per_seq = mod_span // tm
    return pl.pallas_call(
        functools.partial(_inproj_kernel, GATE_COL0 // tn),
        grid=(t // tm, n_cols // tn),
        in_specs=[
            pl.BlockSpec((tm, d), lambda i, j: (i, 0)),
            pl.BlockSpec((None, 4, d), lambda i, j: (layer, 0, 0)),
            pl.BlockSpec((None, 6, d), lambda i, j: (i // per_seq, 0, 0)),
            pl.BlockSpec((None, d, tn), lambda i, j: (layer, 0, j)),
        ],
        out_specs=pl.BlockSpec((tm, tn), lambda i, j: (i, j)),
        out_shape=jax.ShapeDtypeStruct((t, n_cols), BF16),
        scratch_shapes=[pltpu.VMEM((tm, d), BF16)],
        compiler_params=_params("arbitrary", "arbitrary"),
        name="inproj",
    )(x, norm_g, mod, w)


def _ret_tables_kernel(dec_ref, dmat_ref, qdec_ref, kdec_ref, cdec_ref):
    row = lax.broadcasted_iota(jnp.int32, (RET_CHUNK, RET_CHUNK), 0).astype(F32)
    col = lax.broadcasted_iota(jnp.int32, (RET_CHUNK, RET_CHUNK), 1).astype(F32)
    for h in range(RET_HEADS):
        lg_f = _log_sigmoid(jnp.full((1, RET_DK), dec_ref[0, h], F32))
        lg_b = _log_sigmoid(jnp.full((1, RET_DK), dec_ref[1, h], F32))
        dmat_ref[h] = K_SCALE * jnp.where(row >= col, jnp.exp(lg_f * jnp.maximum(row - col, 0.0)),
                                          jnp.exp(lg_b * jnp.maximum(col - row, 0.0)))
        qdec_ref[0, h] = jnp.exp(lg_f * (row + 1.0))
        qdec_ref[1, h] = jnp.exp(lg_b * (RET_CHUNK - row))
        kdec_ref[0, h] = K_SCALE * jnp.exp(lg_f * (RET_CHUNK - 1.0 - row))
        kdec_ref[1, h] = K_SCALE * jnp.exp(lg_b * row)
        cdec_ref[0, h] = jnp.broadcast_to(jnp.exp(lg_f * RET_CHUNK), (SUBLANES, RET_DK))
        cdec_ref[1, h] = jnp.broadcast_to(jnp.exp(lg_b * RET_CHUNK), (SUBLANES, RET_DK))


def _ret_tables(dec):
    sq = (RET_HEADS, RET_CHUNK, RET_CHUNK)
    return pl.pallas_call(
        _ret_tables_kernel,
        in_specs=[pl.BlockSpec(memory_space=pltpu.SMEM)],
        out_shape=[jax.ShapeDtypeStruct(sq, F32), jax.ShapeDtypeStruct((2,) + sq, F32),
                   jax.ShapeDtypeStruct((2,) + sq, F32),
                   jax.ShapeDtypeStruct((2, RET_HEADS, SUBLANES, RET_DK), F32)],
        name="ret_tables",
    )(dec)


def _rope(t, cos, sin):
    return t * cos + pltpu.roll(t, RET_DK // 2, 1) * sin


def _ret_state_kernel(use_rope, cps, n_steps, *refs):
    if use_rope:
        (kdec_ref, cdec_ref, kf_ref, vf_ref, kb_ref, vb_ref, cf_ref, sf_ref, cb_ref, sb_ref, s0f_ref, s0b_ref,
         of_ref, ob_ref, ff_ref, fb_ref, st_ref) = refs
    else:
        (kdec_ref, cdec_ref, kf_ref, vf_ref, kb_ref, vb_ref, s0f_ref, s0b_ref,
         of_ref, ob_ref, ff_ref, fb_ref, st_ref) = refs
    i = pl.program_id(1)

    @pl.when(i == 0)
    def _():
        st_ref[0] = s0f_ref[...]
        st_ref[1] = s0b_ref[...]

    for cc in range(cps):
        rf = slice(cc * RET_CHUNK, (cc + 1) * RET_CHUNK)
        rb = slice((cps - 1 - cc) * RET_CHUNK, (cps - cc) * RET_CHUNK)
        for h in range(RET_HEADS):
            hs = slice(h * RET_DK, (h + 1) * RET_DK)
            kf = kf_ref[rf, hs].astype(F32)
            kb = kb_ref[rb, hs].astype(F32)
            if use_rope:
                kf = _rope(kf, cf_ref[rf, :], sf_ref[rf, :])
                kb = _rope(kb, cb_ref[rb, :], sb_ref[rb, :])
            kf = kf * kdec_ref[0, h]
            kb = kb * kdec_ref[1, h]
            upd_f = jnp.dot(kf.T.astype(BF16), vf_ref[rf, hs], preferred_element_type=F32)
            upd_b = jnp.dot(kb.T.astype(BF16), vb_ref[rb, hs], preferred_element_type=F32)
            s_f = st_ref[0, h]
            s_b = st_ref[1, h]
            of_ref[cc, h] = s_f.astype(of_ref.dtype)
            ob_ref[cps - 1 - cc, h] = s_b.astype(ob_ref.dtype)
            st_ref[0, h] = s_f * cdec_ref[0, h, 0:1, :] + upd_f
            st_ref[1, h] = s_b * cdec_ref[1, h, 0:1, :] + upd_b

    @pl.when(i == n_steps - 1)
    def _():
        ff_ref[...] = st_ref[0]
        fb_ref[...] = st_ref[1]


def _chunks_per_step(n_chunks):
    return math.gcd(n_chunks, 4)


def _ret_state(parts, tables, s0f, s0b, rope_tabs, batch, seq_len):
    n = seq_len // RET_CHUNK
    cps = _chunks_per_step(n)
    ns = n // cps
    rows = cps * RET_CHUNK
    use_rope = rope_tabs is not None
    _, _, kdec, cdec = tables
    blk = lambda col, rev: pl.BlockSpec(
        (rows, BRANCH_W), (lambda b, i: (b * ns + ns - 1 - i, col)) if rev else (lambda b, i: (b * ns + i, col)))
    tab = lambda rev: pl.BlockSpec((rows, RET_DK), (lambda b, i: (ns - 1 - i, 0)) if rev else (lambda b, i: (i, 0)))
    st_in = pl.BlockSpec((None, RET_HEADS, RET_DK, RET_DK), lambda b, i: (b, 0, 0, 0))
    full = lambda a: pl.BlockSpec(a.shape, lambda b, i: (0,) * a.ndim)
    in_specs = [full(kdec), full(cdec), blk(0, False), blk(1, False), blk(0, True), blk(1, True)]
    args = [kdec, cdec, parts, parts, parts, parts]
    if use_rope:
        in_specs += [tab(False), tab(False), tab(True), tab(True)]
        args += [rope_tabs[0], rope_tabs[1], rope_tabs[0], rope_tabs[1]]
    in_specs += [st_in, st_in]
    args += [s0f, s0b]
    st_shape = jax.ShapeDtypeStruct((batch, n, RET_HEADS, RET_DK, RET_DK), BF16)
    fin_shape = jax.ShapeDtypeStruct((batch, RET_HEADS, RET_DK, RET_DK), F32)
    return pl.pallas_call(
        functools.partial(_ret_state_kernel, use_rope, cps, ns),
        grid=(batch, ns),
        in_specs=in_specs,
        out_specs=[
            pl.BlockSpec((None, cps, RET_HEADS, RET_DK, RET_DK), lambda b, i: (b, i, 0, 0, 0)),
            pl.BlockSpec((None, cps, RET_HEADS, RET_DK, RET_DK), lambda b, i: (b, ns - 1 - i, 0, 0, 0)),
            st_in, st_in,
        ],
        out_shape=[st_shape, st_shape, fin_shape, fin_shape],
        scratch_shapes=[pltpu.VMEM((2, RET_HEADS, RET_DK, RET_DK), F32)],
        compiler_params=_params("arbitrary", "arbitrary"),
        name="ret_state",
    )(*args)


def _ret_out_kernel(use_rope, cps, *refs):
    if use_rope:
        dmat_ref, qdec_ref, k_ref, v_ref, q_ref, g_ref, cos_ref, sin_ref, sf_ref, sb_ref, o_ref = refs
    else:
        dmat_ref, qdec_ref, k_ref, v_ref, q_ref, g_ref, sf_ref, sb_ref, o_ref = refs
    for cc in range(cps):
        rs = slice(cc * RET_CHUNK, (cc + 1) * RET_CHUNK)
        for h in range(RET_HEADS):
            hs = slice(h * RET_DK, (h + 1) * RET_DK)
            q = q_ref[rs, hs].astype(F32)
            k = k_ref[rs, hs].astype(F32)
            if use_rope:
                cos, sin = cos_ref[rs, :], sin_ref[rs, :]
                q = _rope(q, cos, sin)
                k = _rope(k, cos, sin)
            scores = lax.dot_general(q.astype(BF16), k.astype(BF16), (((1,), (1,)), ((), ())),
                                     preferred_element_type=F32)
            y = jnp.dot((scores * dmat_ref[h]).astype(BF16), v_ref[rs, hs], preferred_element_type=F32)
            y = y + jnp.dot((q * qdec_ref[0, h]).astype(BF16), sf_ref[cc, h], preferred_element_type=F32)
            y = y + jnp.dot((q * qdec_ref[1, h]).astype(BF16), sb_ref[cc, h], preferred_element_type=F32)
            mu = jnp.mean(y, axis=-1, keepdims=True)
            yc = y - mu
            yn = yc * lax.rsqrt(jnp.mean(yc * yc, axis=-1, keepdims=True) + EPS)
            g = g_ref[rs, hs].astype(F32)
            o_ref[rs, hs] = (yn * (g * _sigmoid(g))).astype(o_ref.dtype)


def _ret_out(parts, tables, st_f, st_b, rope_tabs, batch, seq_len):
    n = seq_len // RET_CHUNK
    cps = _chunks_per_step(n)
    ns = n // cps
    rows = cps * RET_CHUNK
    use_rope = rope_tabs is not None
    dmat, qdec, _, _ = tables
    blk = lambda col: pl.BlockSpec((rows, BRANCH_W), lambda b, i: (b * ns + i, col))
    st = pl.BlockSpec((None, cps, RET_HEADS, RET_DK, RET_DK), lambda b, i: (b, i, 0, 0, 0))
    full = lambda a: pl.BlockSpec(a.shape, lambda b, i: (0,) * a.ndim)
    in_specs = [full(dmat), full(qdec), blk(0), blk(1), blk(2), blk(3)]
    args = [dmat, qdec, parts, parts, parts, parts]
    if use_rope:
        tab = pl.BlockSpec((rows, RET_DK), lambda b, i: (i, 0))
        in_specs += [tab, tab]
        args += list(rope_tabs)
    in_specs += [st, st]
    args += [st_f, st_b]
    return pl.pallas_call(
        functools.partial(_ret_out_kernel, use_rope, cps),
        grid=(batch, ns),
        in_specs=in_specs,
        out_specs=blk(0),
        out_shape=jax.ShapeDtypeStruct((batch * seq_len, BRANCH_W), BF16),
        compiler_params=_params("arbitrary", "arbitrary"),
        name="ret_out",
    )(*args)


def _chan_dft_kernel(x_ref, m_ref, o_ref):
    for g in range(FNET_GROUPS):
        r = jnp.dot(x_ref[:, g * FNET_GW:(g + 1) * FNET_GW], m_ref[...], preferred_element_type=F32)
        o_ref[:, g * FNET_GW:(g + 1) * FNET_GW] = r[:, :FNET_GW]
        o_ref[:, BRANCH_W + g * FNET_GW:BRANCH_W + (g + 1) * FNET_GW] = r[:, FNET_GW:]


def _chan_dft(parts, chan_mat):
    t = parts.shape[0]
    tm = min(1024, t)
    return pl.pallas_call(
        _chan_dft_kernel,
        grid=(t // tm,),
        in_specs=[pl.BlockSpec((tm, BRANCH_W), lambda i: (i, 4)),
                  pl.BlockSpec((FNET_GW, 2 * FNET_GW), lambda i: (0, 0))],
        out_specs=pl.BlockSpec((tm, 2 * BRANCH_W), lambda i: (i, 0)),
        out_shape=jax.ShapeDtypeStruct((t, 2 * BRANCH_W), F32),
        compiler_params=_params("arbitrary"),
        name="chan_dft",
    )(parts, chan_mat)


def _fft_rows_kernel(rows, cols_per_step, x_ref, f_ref, tc_ref, ts_ref, o_ref):
    n = rows * cols_per_step
    x2d = x_ref[...].reshape(n, 2 * BRANCH_W).astype(BF16)
    res = jnp.dot(f_ref[...], x2d, preferred_element_type=F32)
    tc = tc_ref[...]
    ts = ts_ref[...]
    for g in range(FNET_GROUPS):
        u0 = g * FNET_GW
        v0 = u0 + BRANCH_W
        a_re = res[:n, u0:u0 + FNET_GW] - res[n:, v0:v0 + FNET_GW]
        a_im = -(res[:n, v0:v0 + FNET_GW] + res[n:, u0:u0 + FNET_GW])
        gs = slice(g * FNET_GW, (g + 1) * FNET_GW)
        o_ref[0, :, :, gs] = (a_re * tc + a_im * ts).reshape(rows, cols_per_step, FNET_GW)
        o_ref[1, :, :, gs] = (a_im * tc - a_re * ts).reshape(rows, cols_per_step, FNET_GW)


def _fft_rows(uv, f_mat, tw_cos, tw_sin, batch, rows):
    cols_per_step = SUBLANES
    n = rows * cols_per_step
    x = uv.reshape(batch, rows, GRID_W, 2 * BRANCH_W)
    return pl.pallas_call(
        functools.partial(_fft_rows_kernel, rows, cols_per_step),
        grid=(batch, GRID_W // cols_per_step),
        in_specs=[
            pl.BlockSpec((None, rows, cols_per_step, 2 * BRANCH_W), lambda b, j: (b, 0, j, 0)),
            pl.BlockSpec((2 * n, n), lambda b, j: (0, 0)),
            pl.BlockSpec((None, n, FNET_GW), lambda b, j: (j, 0, 0)),
            pl.BlockSpec((None, n, FNET_GW), lambda b, j: (j, 0, 0)),
        ],
        out_specs=pl.BlockSpec((None, 2, rows, cols_per_step, BRANCH_W), lambda b, j: (b, 0, 0, j, 0)),
        out_shape=jax.ShapeDtypeStruct((batch, 2, rows, GRID_W, BRANCH_W), F32),
        compiler_params=_params("arbitrary", "arbitrary"),
        name="fft_rows",
    )(x, f_mat, tw_cos, tw_sin)


def _fft_cols_kernel(scale, k1_per_step, p_ref, f_ref, o_ref):
    for k1 in range(k1_per_step):
        stacked = jnp.concatenate([p_ref[0, k1], p_ref[1, k1]], axis=0).astype(BF16)
        o_ref[:, k1, :] = jnp.dot(f_ref[...], stacked, preferred_element_type=F32) * scale


def _fft_cols(p, f_mat, batch, rows, scale):
    k1_per_step = SUBLANES
    out = pl.pallas_call(
        functools.partial(_fft_cols_kernel, scale, k1_per_step),
        grid=(batch, rows // k1_per_step),
        in_specs=[
            pl.BlockSpec((None, 2, k1_per_step, GRID_W, BRANCH_W), lambda b, j: (b, 0, j, 0, 0)),
            pl.BlockSpec((GRID_W, 2 * GRID_W), lambda b, j: (0, 0)),
        ],
        out_specs=pl.BlockSpec((None, GRID_W, k1_per_step, BRANCH_W), lambda b, j: (b, 0, j, 0)),
        out_shape=jax.ShapeDtypeStruct((batch, GRID_W, rows, BRANCH_W), F32),
        compiler_params=_params("arbitrary", "arbitrary"),
        name="fft_cols",
    )(p, f_mat)
    return out.reshape(batch * GRID_W * rows, BRANCH_W)


def _fnet_dense_kernel(scale, uv_ref, c_ref, s_ref, o_ref):
    y = jnp.dot(c_ref[...], uv_ref[:, :BRANCH_W].astype(BF16), preferred_element_type=F32)
    y = y - jnp.dot(s_ref[...], uv_ref[:, BRANCH_W:].astype(BF16), preferred_element_type=F32)
    o_ref[...] = y * scale


def _fnet_dense(uv, cos_mat, sin_mat, batch, seq_len, scale):
    return pl.pallas_call(
        functools.partial(_fnet_dense_kernel, scale),
        grid=(batch,),
        in_specs=[pl.BlockSpec((seq_len, 2 * BRANCH_W), lambda b: (b, 0)),
                  pl.BlockSpec((seq_len, seq_len), lambda b: (0, 0)),
                  pl.BlockSpec((seq_len, seq_len), lambda b: (0, 0))],
        out_specs=pl.BlockSpec((seq_len, BRANCH_W), lambda b: (b, 0)),
        out_shape=jax.ShapeDtypeStruct((batch * seq_len, BRANCH_W), F32),
        compiler_params=_params("arbitrary"),
        name="fnet_dense",
    )(uv, cos_mat, sin_mat)


def _dft_cos_sin(n):
    idx = np.arange(n)
    ang = 2.0 * np.pi * ((np.outer(idx, idx)) % n) / n
    return np.cos(ang), np.sin(ang)


def _mxu_const(a):
    return jnp.asarray(a, F32).astype(BF16)


def _fourier_mix(parts, batch, seq_len):
    scale = 1.0 / math.sqrt(seq_len * FNET_GW)
    rows = seq_len // GRID_W
    c_ch, s_ch = _dft_cos_sin(FNET_GW)
    uv = _chan_dft(parts, _mxu_const(np.concatenate([c_ch, s_ch], axis=1)))
    if rows % 16 == 0:
        c_r, s_r = _dft_cos_sin(rows)
        eye = np.eye(SUBLANES)
        row_mat = _mxu_const(np.concatenate([np.kron(c_r, eye), np.kron(s_r, eye)], axis=0))
        c_c, s_c = _dft_cos_sin(GRID_W)
        col_mat = _mxu_const(np.concatenate([c_c, s_c], axis=1))
        ang = 2.0 * np.pi * np.outer(np.arange(rows), np.arange(GRID_W)) / seq_len
        ang = ang.reshape(rows, GRID_W // SUBLANES, SUBLANES).transpose(1, 0, 2).reshape(-1, rows * SUBLANES)
        tw_cos = jnp.asarray(np.broadcast_to(np.cos(ang)[:, :, None], ang.shape + (FNET_GW,)), F32)
        tw_sin = jnp.asarray(np.broadcast_to(np.sin(ang)[:, :, None], ang.shape + (FNET_GW,)), F32)
        p = _fft_rows(uv, row_mat, tw_cos, tw_sin, batch, rows)
        return _fft_cols(p, col_mat, batch, rows, scale)
    c_l, s_l = _dft_cos_sin(seq_len)
    return _fnet_dense(uv, _mxu_const(c_l), _mxu_const(s_l), batch, seq_len, scale)


def _conv_kernel(tiles_per_seq, tl, scb_ref, scc_ref, scx_ref, cfa_ref, cfb_ref,
                 scc_p, scx_p, cfa_p, cfb_p, scc_n, scx_n, cfa_n, cfb_n,
                 wsc_ref, wcf_ref, ln_ref, osc_ref, ocf_ref, esc_ref, ecf_ref):
    i = pl.program_id(0)
    keep_prev = jnp.where(i % tiles_per_seq == 0, 0.0, 1.0)
    keep_next = jnp.where(i % tiles_per_seq == tiles_per_seq - 1, 0.0, 1.0)
    ext = tl + 2 * HALO_ROWS

    def gated(a_ref, b_ref):
        return a_ref[...].astype(F32) * _sigmoid(b_ref[...].astype(F32))

    def prod(a_ref, b_ref):
        return a_ref[...].astype(F32) * b_ref[...].astype(F32)

    esc_ref[0:HALO_ROWS] = prod(scc_p, scx_p) * keep_prev
    esc_ref[HALO_ROWS:HALO_ROWS + tl] = prod(scc_ref, scx_ref)
    esc_ref[HALO_ROWS + tl:] = prod(scc_n, scx_n) * keep_next
    ecf_ref[0, 0:HALO_ROWS] = gated(cfa_p, cfb_p) * keep_prev
    ecf_ref[0, HALO_ROWS:HALO_ROWS + tl] = gated(cfa_ref, cfb_ref)
    ecf_ref[0, HALO_ROWS + tl:ext] = gated(cfa_n, cfb_n) * keep_next
    for s in range(1, SUBLANES):
        ecf_ref[s, 0:ext - SUBLANES] = ecf_ref[0, s:ext - SUBLANES + s]

    rs = 32
    lane_groups = BRANCH_W // LANES
    for r in range(tl // rs):
        r0 = r * rs
        accs = []
        for lg in range(lane_groups):
            ls = slice(lg * LANES, (lg + 1) * LANES)
            acc = jnp.zeros((rs, LANES), F32)
            for j in range(SC_WIDTH):
                off = HALO_ROWS + r0 + j - (SC_WIDTH - 1) // 2
                acc = acc + wsc_ref[j:j + 1, ls] * esc_ref[off:off + rs, ls]
            osc_ref[r0:r0 + rs, ls] = (scb_ref[r0:r0 + rs, ls].astype(F32) * acc).astype(osc_ref.dtype)
            acc = jnp.zeros((rs, LANES), F32)
            for j in range(CF_WIDTH):
                off = HALO_ROWS + r0 + j - (CF_WIDTH - 1) // 2
                s = off % SUBLANES
                acc = acc + wcf_ref[j:j + 1, ls] * ecf_ref[s, off - s:off - s + rs, ls]
            accs.append(acc)
        tot = accs[0]
        for a in accs[1:]:
            tot = tot + a
        mu = jnp.sum(tot, axis=-1, keepdims=True) * (1.0 / BRANCH_W)
        sq = None
        for a in accs:
            d = a - mu
            sq = d * d if sq is None else sq + d * d
        inv = lax.rsqrt(jnp.sum(sq, axis=-1, keepdims=True) * (1.0 / BRANCH_W) + EPS)
        for lg, a in enumerate(accs):
            ls = slice(lg * LANES, (lg + 1) * LANES)
            z = (a - mu) * inv * ln_ref[0:1, ls] + ln_ref[1:2, ls]
            ocf_ref[r0:r0 + rs, ls] = (z * _sigmoid(z)).astype(ocf_ref.dtype)


def _conv_branches(parts, sc_conv, cf_conv, cf_ln, layer, seq_len):
    t = parts.shape[0]
    tl = min(256, seq_len)
    assert seq_len % tl == 0 and tl % HALO_ROWS == 0
    tiles_per_seq = seq_len // tl
    halo_per_tile = tl // HALO_ROWS
    n_halo = t // HALO_ROWS
    main = lambda col: pl.BlockSpec((tl, BRANCH_W), lambda i: (i, col))
    prev = lambda col: pl.BlockSpec((HALO_ROWS, BRANCH_W), lambda i: (jnp.maximum(i * halo_per_tile - 1, 0), col))
    nxt = lambda col: pl.BlockSpec((HALO_ROWS, BRANCH_W),
                                   lambda i: (jnp.minimum((i + 1) * halo_per_tile, n_halo - 1), col))
    per_layer = lambda a: pl.BlockSpec((None,) + a.shape[1:], lambda i: (layer, 0, 0))
    out = pl.BlockSpec((tl, BRANCH_W), lambda i: (i, 0))
    ext = tl + 2 * HALO_ROWS
    return pl.pallas_call(
        functools.partial(_conv_kernel, tiles_per_seq, tl),
        grid=(t // tl,),
        in_specs=[main(5), main(6), main(7), main(8), main(9),
                  prev(6), prev(7), prev(8), prev(9), nxt(6), nxt(7), nxt(8), nxt(9),
                  per_layer(sc_conv), per_layer(cf_conv), per_layer(cf_ln)],
        out_specs=[out, out],
        out_shape=[jax.ShapeDtypeStruct((t, BRANCH_W), BF16)] * 2,
        scratch_shapes=[pltpu.VMEM((ext, BRANCH_W), F32), pltpu.VMEM((SUBLANES, ext, BRANCH_W), F32)],
        compiler_params=_params("arbitrary"),
        name="conv_branches",
    )(*([parts] * 13), sc_conv, cf_conv, cf_ln)


MIX_COL_TILE = 512
MIX_GATE_BLOCK = 1024


def _mix_kernel(*refs):
    y_refs = refs[:N_BRANCH]
    gates_per_branch = D_MODEL // MIX_GATE_BLOCK
    g_refs = refs[N_BRANCH:N_BRANCH + N_BRANCH * gates_per_branch]
    wb_ref, wo_ref, x_ref, ng_ref, mod_ref, xo_ref, h_ref, m_ref = refs[N_BRANCH + N_BRANCH * gates_per_branch:]
    ys = [y_ref[...].astype(BF16) for y_ref in y_refs]
    for j in range(D_MODEL // MIX_COL_TILE):
        c0 = j * MIX_COL_TILE
        cs = slice(c0, c0 + MIX_COL_TILE)
        merged = None
        for n in range(N_BRANCH):
            p = jnp.dot(ys[n], wb_ref[n, :, cs], preferred_element_type=F32)
            g_ref = g_refs[n * gates_per_branch + c0 // MIX_GATE_BLOCK]
            g0 = c0 % MIX_GATE_BLOCK
            term = g_ref[:, g0:g0 + MIX_COL_TILE].astype(F32) * p
            merged = term if merged is None else merged + term
        m_ref[:, cs] = merged.astype(BF16)
    mix = jnp.dot(m_ref[...], wo_ref[...], preferred_element_type=F32)
    x1 = x_ref[...] + mod_ref[2:3, :] * (_rms(mix) * ng_ref[1:2, :])
    xo_ref[...] = x1
    h = _rms(x1) * ng_ref[2:3, :]
    h_ref[...] = (h * (1.0 + mod_ref[4:5, :]) + mod_ref[3:4, :]).astype(h_ref.dtype)


def _mix(ys, parts, w_branch, w_out, x, norm_g, mod, layer, mod_span):
    t, d = x.shape
    tm = min(256, mod_span)
    per_seq = mod_span // tm
    gate0 = GATE_COL0 // MIX_GATE_BLOCK
    n_gate_blocks = N_BRANCH * d // MIX_GATE_BLOCK
    resident = pl.Buffered(1)
    y_spec = pl.BlockSpec((tm, BRANCH_W), lambda i: (i, 0))
    gate = lambda k: pl.BlockSpec((tm, MIX_GATE_BLOCK), lambda i: (i, gate0 + k))
    row_tile = pl.BlockSpec((tm, d), lambda i: (i, 0))
    return pl.pallas_call(
        _mix_kernel,
        grid=(t // tm,),
        in_specs=[y_spec] * N_BRANCH + [gate(k) for k in range(n_gate_blocks)] + [
            pl.BlockSpec((None, N_BRANCH, BRANCH_W, d), lambda i: (layer, 0, 0, 0), pipeline_mode=resident),
            pl.BlockSpec((None, d, d), lambda i: (layer, 0, 0), pipeline_mode=resident),
            row_tile,
            pl.BlockSpec((None, 4, d), lambda i: (layer, 0, 0)),
            pl.BlockSpec((None, 6, d), lambda i: (i // per_seq, 0, 0))],
        out_specs=[row_tile, row_tile],
        out_shape=[jax.ShapeDtypeStruct((t, d), F32), jax.ShapeDtypeStruct((t, d), BF16)],
        scratch_shapes=[pltpu.VMEM((tm, d), BF16)],
        compiler_params=_params("arbitrary"),
        name="mix",
    )(*ys, *([parts] * n_gate_blocks), w_branch, w_out, x, norm_g, mod)


def _ffn_kernel(n_f, h_ref, w1_ref, w2_ref, x_ref, ng_ref, mod_ref, o_ref, acc_ref):
    f = pl.program_id(1)

    @pl.when(f == 0)
    def _():
        acc_ref[...] = jnp.zeros_like(acc_ref)

    a = jnp.maximum(jnp.dot(h_ref[...], w1_ref[...], preferred_element_type=F32), 0.0)
    acc_ref[...] += jnp.dot((a * a).astype(BF16), w2_ref[...], preferred_element_type=F32)

    @pl.when(f == n_f - 1)
    def _():
        o_ref[...] = x_ref[...] + mod_ref[5:6, :] * (_rms(acc_ref[...]) * ng_ref[3:4, :])


def _ffn(h, w1, w2, x, norm_g, mod, layer, mod_span):
    t, d = x.shape
    d_ff = w1.shape[2]
    tm = min(512, mod_span)
    tf = 1024
    per_seq = mod_span // tm
    n_f = d_ff // tf
    row_tile = pl.BlockSpec((tm, d), lambda i, f: (i, 0))
    return pl.pallas_call(
        functools.partial(_ffn_kernel, n_f),
        grid=(t // tm, n_f),
        in_specs=[
            row_tile,
            pl.BlockSpec((None, d, tf), lambda i, f: (layer, 0, f)),
            pl.BlockSpec((None, tf, d), lambda i, f: (layer, f, 0)),
            row_tile,
            pl.BlockSpec((None, 4, d), lambda i, f: (layer, 0, 0)),
            pl.BlockSpec((None, 6, d), lambda i, f: (i // per_seq, 0, 0)),
        ],
        out_specs=row_tile,
        out_shape=jax.ShapeDtypeStruct((t, d), F32),
        scratch_shapes=[pltpu.VMEM((tm, d), F32)],
        compiler_params=_params("arbitrary", "arbitrary"),
        name="ffn",
    )(h, w1, w2, x, norm_g, mod)


def _rope_tables(seq_len):
    rows = seq_len // GRID_W
    row = jnp.repeat(jnp.arange(rows, dtype=F32), GRID_W)
    col = jnp.tile(jnp.arange(GRID_W, dtype=F32), rows)
    freqs = ROPE_BASE ** (-jnp.arange(ROPE_PAIRS, dtype=F32) / ROPE_PAIRS)
    ang = jnp.concatenate([row[:, None] * freqs[None], col[:, None] * freqs[None]], axis=-1)
    cos, sin = jnp.cos(ang), jnp.sin(ang)
    return jnp.concatenate([cos, cos], axis=-1), jnp.concatenate([-sin, sin], axis=-1)


def kernel(x, c, ctx, c_ctx, w_ada, b_ada, norm_g, w_in, ret_decay, sc_conv, cf_conv, cf_ln,
           w_branch, w_out, w_ff1, w_ff2):
    batch, seq_len, d = x.shape
    ctx_len = ctx.shape[1]
    depth = w_ada.shape[0]
    in_cols = w_in.shape[2]
    assert d == D_MODEL and seq_len % (GRID_W * 16) == 0 and ctx_len % RET_CHUNK == 0

    n_rows = -(-(batch + 1) // BF16_ROWS) * BF16_ROWS
    c_rows = jnp.concatenate([c, c_ctx[None], jnp.zeros((n_rows - batch - 1, d), F32)], axis=0)
    mod_all = _ada(c_rows, w_ada, b_ada).reshape(depth, n_rows, 6, d)

    rope_tabs = _rope_tables(seq_len)
    xf = x.reshape(batch * seq_len, d)
    xc = ctx.reshape(batch * ctx_len, d)
    zeros_state = jnp.zeros((batch, RET_HEADS, RET_DK, RET_DK), F32)
    w_in_b, wb_b, wo_b, w1_b, w2_b = (w.astype(BF16) for w in (w_in, w_branch, w_out, w_ff1, w_ff2))

    def mixer_tail(parts, y_ret, x_res, mod, layer, length, mod_span):
        y_f = _fourier_mix(parts, batch, length)
        y_sc, y_cf = _conv_branches(parts, sc_conv, cf_conv, cf_ln, layer, length)
        x1, h2 = _mix((y_ret, y_f, y_sc, y_cf), parts, wb_b, wo_b, x_res, norm_g, mod, layer, mod_span)
        return _ffn(h2, w1_b, w2_b, x1, norm_g, mod, layer, mod_span)

    for layer in range(depth):
        with_ctx = layer < depth - 1
        mod = mod_all[layer, :batch]
        mod_c = mod_all[layer, batch:batch + 1]
        tables = _ret_tables(ret_decay[layer].astype(F32))

        ctx_cols = in_cols if with_ctx else 2 * BRANCH_W
        parts_c = _inproj(xc, norm_g, mod_c, w_in_b, layer, ctx_cols, batch * ctx_len)
        stc_f, stc_b, s0f, s0b = _ret_state(parts_c, tables, zeros_state, zeros_state, None, batch, ctx_len)

        parts = _inproj(xf, norm_g, mod, w_in_b, layer, in_cols, seq_len)
        st_f, st_b, _, _ = _ret_state(parts, tables, s0f, s0b, rope_tabs, batch, seq_len)
        y_ret = _ret_out(parts, tables, st_f, st_b, rope_tabs, batch, seq_len)
        xf = mixer_tail(parts, y_ret, xf, mod, layer, seq_len, seq_len)

        if with_ctx:
            y_ret_c = _ret_out(parts_c, tables, stc_f, stc_b, None, batch, ctx_len)
            xc = mixer_tail(parts_c, y_ret_c, xc, mod_c, layer, ctx_len, batch * ctx_len)
    return xf.reshape(batch, seq_len, d)
```

```python
import functools
import math

import numpy as np
import jax
import jax.numpy as jnp
from jax import lax
from jax.experimental import pallas as pl
from jax.experimental.pallas import tpu as pltpu

F32 = jnp.float32
BF16 = jnp.bfloat16

D_MODEL = 2048
GRID_W = 64
N_BRANCH = 4
BRANCH_W = D_MODEL // 4
RET_HEADS = 4
RET_DK = BRANCH_W // RET_HEADS
RET_CHUNK = 128
ROPE_PAIRS = RET_DK // 4
ROPE_BASE = 10000.0
FNET_GROUPS = 4
FNET_GW = BRANCH_W // FNET_GROUPS
SC_WIDTH = 3
CF_WIDTH = 31
GATE_COL0 = 10 * BRANCH_W
EPS = 1e-6
K_SCALE = RET_DK ** -0.5

VMEM_LIMIT_BYTES = 56 * 1024 * 1024
SUBLANES = 8
BF16_ROWS = 16
HALO_ROWS = BF16_ROWS
LANES = 128


def _params(*sem):
    return pltpu.CompilerParams(dimension_semantics=sem, vmem_limit_bytes=VMEM_LIMIT_BYTES)


def _sigmoid(v):
    return 0.5 * jnp.tanh(0.5 * v) + 0.5


def _log_sigmoid(v):
    return jnp.minimum(v, 0.0) - jnp.log1p(jnp.exp(-jnp.abs(v)))


def _rms(v):
    return v * lax.rsqrt(jnp.mean(v * v, axis=-1, keepdims=True) + EPS)


def _ada_kernel(c_ref, w_ref, b_ref, o_ref):
    c = c_ref[...]
    s = c * _sigmoid(c)
    w = w_ref[...]
    s_hi = s.astype(BF16)
    s_lo = (s - s_hi.astype(F32)).astype(BF16)
    w_hi = w.astype(BF16)
    w_lo = (w - w_hi.astype(F32)).astype(BF16)
    acc = jnp.dot(s_hi, w_hi, preferred_element_type=F32)
    acc = acc + jnp.dot(s_hi, w_lo, preferred_element_type=F32)
    acc = acc + jnp.dot(s_lo, w_hi, preferred_element_type=F32)
    o_ref[...] = acc + b_ref[...]


def _ada(c_rows, w_ada, b_ada):
    depth, d, n = w_ada.shape
    rows = c_rows.shape[0]
    tn = 1024
    return pl.pallas_call(
        _ada_kernel,
        grid=(depth, n // tn),
        in_specs=[
            pl.BlockSpec((rows, d), lambda l, j: (0, 0)),
            pl.BlockSpec((None, d, tn), lambda l, j: (l, 0, j)),
            pl.BlockSpec((None, 1, tn), lambda l, j: (l, 0, j)),
        ],
        out_specs=pl.BlockSpec((None, rows, tn), lambda l, j: (l, 0, j)),
        out_shape=jax.ShapeDtypeStruct((depth, rows, n), F32),
        compiler_params=_params("arbitrary", "arbitrary"),
        name="ada",
    )(c_rows, w_ada, b_ada.reshape(depth, 1, n))


def _cast_jobs(mats, n_steps):
    jobs = []
    for a in mats:
        rows = a.shape[0]
        rb = BF16_ROWS
        while rows // rb > n_steps:
            rb *= 2
        assert rows % rb == 0
        jobs.append((a, rb, rows // rb))
    return jobs


def _cast_specs(jobs, step_of):
    specs = [pl.BlockSpec((rb, a.shape[1]), lambda *g, nb=nb: (jnp.minimum(step_of(*g), nb - 1), 0))
             for a, rb, nb in jobs]
    return specs, [jax.ShapeDtypeStruct(a.shape, BF16) for a, _, _ in jobs]


def _cast_step(step, jobs_blocks, src_refs, dst_refs):
    for nb, src, dst in zip(jobs_blocks, src_refs, dst_refs):
        @pl.when(step < nb)
        def _(src=src, dst=dst):
            dst[...] = src[...].astype(BF16)


def _inproj_kernel(gate_tile0, n_j, cast_blocks, x_ref, g_ref, mod_ref, w_ref, *rest):
    n_cast = len(cast_blocks)
    cast_src, o_ref, cast_dst, h_ref = rest[:n_cast], rest[n_cast], rest[n_cast + 1:2 * n_cast + 1], rest[-1]
    j = pl.program_id(1)
    _cast_step(pl.program_id(0) * n_j + j, cast_blocks, cast_src, cast_dst)

    @pl.when(j == 0)
    def _():
        y = _rms(x_ref[...]) * g_ref[0:1, :]
        h = y * (1.0 + mod_ref[1:2, :]) + mod_ref[0:1, :]
        h_ref[...] = h.astype(BF16)

    @pl.when(j < gate_tile0)
    def _():
        o_ref[...] = jnp.dot(h_ref[...], w_ref[...], preferred_element_type=F32).astype(o_ref.dtype)

    @pl.when(j >= gate_tile0)
    def _():
        o_ref[...] = _sigmoid(jnp.dot(h_ref[...], w_ref[...], preferred_element_type=F32)).astype(o_ref.dtype)


def _inproj(x, norm_g, mod, w, layer, n_cols, mod_span, cast_mats=()):
    t, d = x.shape
    tm = min(1024, mod_span)
    tn = min(1024, n_cols)
    assert GATE_COL0 % tn == 0 or n_cols <= GATE_COL0
    per_seq = mod_span // tm
    n_i, n_j = t // tm, n_cols // tn
    jobs = _cast_jobs(cast_mats, n_i * n_j)
    cast_specs, cast_shapes = _cast_specs(jobs, lambda i, j: i * n_j + j)
    out = pl.pallas_call(
        functools.partial(_inproj_kernel, GATE_COL0 // tn, n_j, tuple(nb for _, _, nb in jobs)),
        grid=(n_i, n_j),
        in_specs=[
            pl.BlockSpec((tm, d), lambda i, j: (i, 0)),
            pl.BlockSpec((None, 4, d), lambda i, j: (layer, 0, 0)),
            pl.BlockSpec((None, 6, d), lambda i, j: (i // per_seq, 0, 0)),
            pl.BlockSpec((d, tn), lambda i, j: (0, j)),
        ] + cast_specs,
        out_specs=[pl.BlockSpec((tm, tn), lambda i, j: (i, j))] + cast_specs,
        out_shape=[jax.ShapeDtypeStruct((t, n_cols), BF16)] + cast_shapes,
        scratch_shapes=[pltpu.VMEM((tm, d), BF16)],
        compiler_params=_params("arbitrary", "arbitrary"),
        name="inproj",
    )(x, norm_g, mod, w, *[a for a, _, _ in jobs])
    return out[0], out[1:]


def _ret_tables_kernel(dec_ref, dmat_ref, qdec_ref, kdec_ref, cdec_ref):
    row = lax.broadcasted_iota(jnp.int32, (RET_CHUNK, RET_CHUNK), 0).astype(F32)
    col = lax.broadcasted_iota(jnp.int32, (RET_CHUNK, RET_CHUNK), 1).astype(F32)
    for h in range(RET_HEADS):
        lg_f = _log_sigmoid(jnp.full((1, RET_DK), dec_ref[0, h], F32))
        lg_b = _log_sigmoid(jnp.full((1, RET_DK), dec_ref[1, h], F32))
        dmat_ref[h] = K_SCALE * jnp.where(row >= col, jnp.exp(lg_f * jnp.maximum(row - col, 0.0)),
                                          jnp.exp(lg_b * jnp.maximum(col - row, 0.0)))
        qdec_ref[0, h] = jnp.exp(lg_f * (row + 1.0))
        qdec_ref[1, h] = jnp.exp(lg_b * (RET_CHUNK - row))
        kdec_ref[0, h] = K_SCALE * jnp.exp(lg_f * (RET_CHUNK - 1.0 - row))
        kdec_ref[1, h] = K_SCALE * jnp.exp(lg_b * row)
        cdec_ref[0, h] = jnp.broadcast_to(jnp.exp(lg_f * RET_CHUNK), (SUBLANES, RET_DK))
        cdec_ref[1, h] = jnp.broadcast_to(jnp.exp(lg_b * RET_CHUNK), (SUBLANES, RET_DK))


def _ret_tables(dec):
    sq = (RET_HEADS, RET_CHUNK, RET_CHUNK)
    return pl.pallas_call(
        _ret_tables_kernel,
        in_specs=[pl.BlockSpec(memory_space=pltpu.SMEM)],
        out_shape=[jax.ShapeDtypeStruct(sq, F32), jax.ShapeDtypeStruct((2,) + sq, F32),
                   jax.ShapeDtypeStruct((2,) + sq, F32),
                   jax.ShapeDtypeStruct((2, RET_HEADS, SUBLANES, RET_DK), F32)],
        name="ret_tables",
    )(dec)


def _rope(t, cos, sin):
    return t * cos + pltpu.roll(t, RET_DK // 2, 1) * sin


def _ret_state_kernel(use_rope, cps, n_steps, *refs):
    if use_rope:
        (kdec_ref, cdec_ref, kf_ref, vf_ref, kb_ref, vb_ref, cf_ref, sf_ref, cb_ref, sb_ref, s0f_ref, s0b_ref,
         of_ref, ob_ref, ff_ref, fb_ref, st_ref) = refs
    else:
        (kdec_ref, cdec_ref, kf_ref, vf_ref, kb_ref, vb_ref, s0f_ref, s0b_ref,
         of_ref, ob_ref, ff_ref, fb_ref, st_ref) = refs
    i = pl.program_id(1)

    @pl.when(i == 0)
    def _():
        st_ref[0] = s0f_ref[...]
        st_ref[1] = s0b_ref[...]

    for cc in range(cps):
        rf = slice(cc * RET_CHUNK, (cc + 1) * RET_CHUNK)
        rb = slice((cps - 1 - cc) * RET_CHUNK, (cps - cc) * RET_CHUNK)
        for h in range(RET_HEADS):
            hs = slice(h * RET_DK, (h + 1) * RET_DK)
            kf = kf_ref[rf, hs].astype(F32)
            kb = kb_ref[rb, hs].astype(F32)
            if use_rope:
                kf = _rope(kf, cf_ref[rf, :], sf_ref[rf, :])
                kb = _rope(kb, cb_ref[rb, :], sb_ref[rb, :])
            kf = kf * kdec_ref[0, h]
            kb = kb * kdec_ref[1, h]
            upd_f = jnp.dot(kf.T.astype(BF16), vf_ref[rf, hs], preferred_element_type=F32)
            upd_b = jnp.dot(kb.T.astype(BF16), vb_ref[rb, hs], preferred_element_type=F32)
            s_f = st_ref[0, h]
            s_b = st_ref[1, h]
            of_ref[cc, h] = s_f.astype(of_ref.dtype)
            ob_ref[cps - 1 - cc, h] = s_b.astype(ob_ref.dtype)
            st_ref[0, h] = s_f * cdec_ref[0, h, 0:1, :] + upd_f
            st_ref[1, h] = s_b * cdec_ref[1, h, 0:1, :] + upd_b

    @pl.when(i == n_steps - 1)
    def _():
        ff_ref[...] = st_ref[0]
        fb_ref[...] = st_ref[1]


def _chunks_per_step(n_chunks):
    return math.gcd(n_chunks, 8)


def _ret_state(parts, tables, s0f, s0b, rope_tabs, batch, seq_len):
    n = seq_len // RET_CHUNK
    cps = _chunks_per_step(n)
    ns = n // cps
    rows = cps * RET_CHUNK
    use_rope = rope_tabs is not None
    _, _, kdec, cdec = tables
    blk = lambda col, rev: pl.BlockSpec(
        (rows, BRANCH_W), (lambda b, i: (b * ns + ns - 1 - i, col)) if rev else (lambda b, i: (b * ns + i, col)))
    tab = lambda rev: pl.BlockSpec((rows, RET_DK), (lambda b, i: (ns - 1 - i, 0)) if rev else (lambda b, i: (i, 0)))
    st_in = pl.BlockSpec((None, RET_HEADS, RET_DK, RET_DK), lambda b, i: (b, 0, 0, 0))
    full = lambda a: pl.BlockSpec(a.shape, lambda b, i: (0,) * a.ndim)
    in_specs = [full(kdec), full(cdec), blk(0, False), blk(1, False), blk(0, True), blk(1, True)]
    args = [kdec, cdec, parts, parts, parts, parts]
    if use_rope:
        in_specs += [tab(False), tab(False), tab(True), tab(True)]
        args += [rope_tabs[0], rope_tabs[1], rope_tabs[0], rope_tabs[1]]
    in_specs += [st_in, st_in]
    args += [s0f, s0b]
    st_shape = jax.ShapeDtypeStruct((batch, n, RET_HEADS, RET_DK, RET_DK), BF16)
    fin_shape = jax.ShapeDtypeStruct((batch, RET_HEADS, RET_DK, RET_DK), F32)
    return pl.pallas_call(
        functools.partial(_ret_state_kernel, use_rope, cps, ns),
        grid=(batch, ns),
        in_specs=in_specs,
        out_specs=[
            pl.BlockSpec((None, cps, RET_HEADS, RET_DK, RET_DK), lambda b, i: (b, i, 0, 0, 0)),
            pl.BlockSpec((None, cps, RET_HEADS, RET_DK, RET_DK), lambda b, i: (b, ns - 1 - i, 0, 0, 0)),
            st_in, st_in,
        ],
        out_shape=[st_shape, st_shape, fin_shape, fin_shape],
        scratch_shapes=[pltpu.VMEM((2, RET_HEADS, RET_DK, RET_DK), F32)],
        compiler_params=_params("arbitrary", "arbitrary"),
        name="ret_state",
    )(*args)


def _ret_out_kernel(use_rope, cps, *refs):
    if use_rope:
        dmat_ref, qdec_ref, k_ref, v_ref, q_ref, g_ref, cos_ref, sin_ref, sf_ref, sb_ref, o_ref = refs
    else:
        dmat_ref, qdec_ref, k_ref, v_ref, q_ref, g_ref, sf_ref, sb_ref, o_ref = refs
    for cc in range(cps):
        rs = slice(cc * RET_CHUNK, (cc + 1) * RET_CHUNK)
        for h in range(RET_HEADS):
            hs = slice(h * RET_DK, (h + 1) * RET_DK)
            q = q_ref[rs, hs].astype(F32)
            k = k_ref[rs, hs].astype(F32)
            if use_rope:
                cos, sin = cos_ref[rs, :], sin_ref[rs, :]
                q = _rope(q, cos, sin)
                k = _rope(k, cos, sin)
            scores = lax.dot_general(q.astype(BF16), k.astype(BF16), (((1,), (1,)), ((), ())),
                                     preferred_element_type=F32)
            y = jnp.dot((scores * dmat_ref[h]).astype(BF16), v_ref[rs, hs], preferred_element_type=F32)
            y = y + jnp.dot((q * qdec_ref[0, h]).astype(BF16), sf_ref[cc, h], preferred_element_type=F32)
            y = y + jnp.dot((q * qdec_ref[1, h]).astype(BF16), sb_ref[cc, h], preferred_element_type=F32)
            mu = jnp.mean(y, axis=-1, keepdims=True)
            yc = y - mu
            yn = yc * lax.rsqrt(jnp.mean(yc * yc, axis=-1, keepdims=True) + EPS)
            g = g_ref[rs, hs].astype(F32)
            o_ref[rs, hs] = (yn * (g * _sigmoid(g))).astype(o_ref.dtype)


def _ret_out(parts, tables, st_f, st_b, rope_tabs, batch, seq_len):
    n = seq_len // RET_CHUNK
    cps = _chunks_per_step(n)
    ns = n // cps
    rows = cps * RET_CHUNK
    use_rope = rope_tabs is not None
    dmat, qdec, _, _ = tables
    blk = lambda col: pl.BlockSpec((rows, BRANCH_W), lambda b, i: (b * ns + i, col))
    st = pl.BlockSpec((None, cps, RET_HEADS, RET_DK, RET_DK), lambda b, i: (b, i, 0, 0, 0))
    full = lambda a: pl.BlockSpec(a.shape, lambda b, i: (0,) * a.ndim)
    in_specs = [full(dmat), full(qdec), blk(0), blk(1), blk(2), blk(3)]
    args = [dmat, qdec, parts, parts, parts, parts]
    if use_rope:
        tab = pl.BlockSpec((rows, RET_DK), lambda b, i: (i, 0))
        in_specs += [tab, tab]
        args += list(rope_tabs)
    in_specs += [st, st]
    args += [st_f, st_b]
    return pl.pallas_call(
        functools.partial(_ret_out_kernel, use_rope, cps),
        grid=(batch, ns),
        in_specs=in_specs,
        out_specs=blk(0),
        out_shape=jax.ShapeDtypeStruct((batch * seq_len, BRANCH_W), BF16),
        compiler_params=_params("arbitrary", "arbitrary"),
        name="ret_out",
    )(*args)


def _chan_dft_kernel(x_ref, m_ref, o_ref):
    for g in range(FNET_GROUPS):
        r = jnp.dot(x_ref[:, g * FNET_GW:(g + 1) * FNET_GW], m_ref[...], preferred_element_type=F32)
        o_ref[:, g * FNET_GW:(g + 1) * FNET_GW] = r[:, :FNET_GW]
        o_ref[:, BRANCH_W + g * FNET_GW:BRANCH_W + (g + 1) * FNET_GW] = r[:, FNET_GW:]


def _chan_dft(parts, chan_mat):
    t = parts.shape[0]
    tm = min(1024, t)
    return pl.pallas_call(
        _chan_dft_kernel,
        grid=(t // tm,),
        in_specs=[pl.BlockSpec((tm, BRANCH_W), lambda i: (i, 4)),
                  pl.BlockSpec((FNET_GW, 2 * FNET_GW), lambda i: (0, 0))],
        out_specs=pl.BlockSpec((tm, 2 * BRANCH_W), lambda i: (i, 0)),
        out_shape=jax.ShapeDtypeStruct((t, 2 * BRANCH_W), F32),
        compiler_params=_params("arbitrary"),
        name="chan_dft",
    )(parts, chan_mat)


def _fft_rows_kernel(rows, cols_per_step, x_ref, f_ref, tc_ref, ts_ref, o_ref):
    n = rows * cols_per_step
    x2d = x_ref[...].reshape(n, 2 * BRANCH_W).astype(BF16)
    res = jnp.dot(f_ref[...], x2d, preferred_element_type=F32)
    tc = tc_ref[...]
    ts = ts_ref[...]
    for g in range(FNET_GROUPS):
        u0 = g * FNET_GW
        v0 = u0 + BRANCH_W
        a_re = res[:n, u0:u0 + FNET_GW] - res[n:, v0:v0 + FNET_GW]
        a_im = -(res[:n, v0:v0 + FNET_GW] + res[n:, u0:u0 + FNET_GW])
        gs = slice(g * FNET_GW, (g + 1) * FNET_GW)
        o_ref[0, :, :, gs] = (a_re * tc + a_im * ts).reshape(rows, cols_per_step, FNET_GW)
        o_ref[1, :, :, gs] = (a_im * tc - a_re * ts).reshape(rows, cols_per_step, FNET_GW)


def _fft_rows(uv, f_mat, tw_cos, tw_sin, batch, rows):
    cols_per_step = SUBLANES
    n = rows * cols_per_step
    x = uv.reshape(batch, rows, GRID_W, 2 * BRANCH_W)
    return pl.pallas_call(
        functools.partial(_fft_rows_kernel, rows, cols_per_step),
        grid=(batch, GRID_W // cols_per_step),
        in_specs=[
            pl.BlockSpec((None, rows, cols_per_step, 2 * BRANCH_W), lambda b, j: (b, 0, j, 0)),
            pl.BlockSpec((2 * n, n), lambda b, j: (0, 0)),
            pl.BlockSpec((None, n, FNET_GW), lambda b, j: (j, 0, 0)),
            pl.BlockSpec((None, n, FNET_GW), lambda b, j: (j, 0, 0)),
        ],
        out_specs=pl.BlockSpec((None, 2, rows, cols_per_step, BRANCH_W), lambda b, j: (b, 0, 0, j, 0)),
        out_shape=jax.ShapeDtypeStruct((batch, 2, rows, GRID_W, BRANCH_W), F32),
        compiler_params=_params("arbitrary", "arbitrary"),
        name="fft_rows",
    )(x, f_mat, tw_cos, tw_sin)


def _fft_cols_kernel(scale, k1_per_step, p_ref, f_ref, o_ref):
    for k1 in range(k1_per_step):
        stacked = jnp.concatenate([p_ref[0, k1], p_ref[1, k1]], axis=0).astype(BF16)
        o_ref[:, k1, :] = jnp.dot(f_ref[...], stacked, preferred_element_type=F32) * scale


def _fft_cols(p, f_mat, batch, rows, scale):
    k1_per_step = SUBLANES
    out = pl.pallas_call(
        functools.partial(_fft_cols_kernel, scale, k1_per_step),
        grid=(batch, rows // k1_per_step),
        in_specs=[
            pl.BlockSpec((None, 2, k1_per_step, GRID_W, BRANCH_W), lambda b, j: (b, 0, j, 0, 0)),
            pl.BlockSpec((GRID_W, 2 * GRID_W), lambda b, j: (0, 0)),
        ],
        out_specs=pl.BlockSpec((None, GRID_W, k1_per_step, BRANCH_W), lambda b, j: (b, 0, j, 0)),
        out_shape=jax.ShapeDtypeStruct((batch, GRID_W, rows, BRANCH_W), F32),
        compiler_params=_params("arbitrary", "arbitrary"),
        name="fft_cols",
    )(p, f_mat)
    return out.reshape(batch * GRID_W * rows, BRANCH_W)


def _fnet_dense_kernel(scale, uv_ref, c_ref, s_ref, o_ref):
    y = jnp.dot(c_ref[...], uv_ref[:, :BRANCH_W].astype(BF16), preferred_element_type=F32)
    y = y - jnp.dot(s_ref[...], uv_ref[:, BRANCH_W:].astype(BF16), preferred_element_type=F32)
    o_ref[...] = y * scale


def _fnet_dense(uv, cos_mat, sin_mat, batch, seq_len, scale):
    return pl.pallas_call(
        functools.partial(_fnet_dense_kernel, scale),
        grid=(batch,),
        in_specs=[pl.BlockSpec((seq_len, 2 * BRANCH_W), lambda b: (b, 0)),
                  pl.BlockSpec((seq_len, seq_len), lambda b: (0, 0)),
                  pl.BlockSpec((seq_len, seq_len), lambda b: (0, 0))],
        out_specs=pl.BlockSpec((seq_len, BRANCH_W), lambda b: (b, 0)),
        out_shape=jax.ShapeDtypeStruct((batch * seq_len, BRANCH_W), F32),
        compiler_params=_params("arbitrary"),
        name="fnet_dense",
    )(uv, cos_mat, sin_mat)


def _dft_cos_sin(n):
    idx = np.arange(n)
    ang = 2.0 * np.pi * ((np.outer(idx, idx)) % n) / n
    return np.cos(ang), np.sin(ang)


def _mxu_const(a):
    return jnp.asarray(a, F32).astype(BF16)


def _fourier_mix(parts, batch, seq_len):
    scale = 1.0 / math.sqrt(seq_len * FNET_GW)
    rows = seq_len // GRID_W
    c_ch, s_ch = _dft_cos_sin(FNET_GW)
    uv = _chan_dft(parts, _mxu_const(np.concatenate([c_ch, s_ch], axis=1)))
    if rows % 16 == 0:
        c_r, s_r = _dft_cos_sin(rows)
        eye = np.eye(SUBLANES)
        row_mat = _mxu_const(np.concatenate([np.kron(c_r, eye), np.kron(s_r, eye)], axis=0))
        c_c, s_c = _dft_cos_sin(GRID_W)
        col_mat = _mxu_const(np.concatenate([c_c, s_c], axis=1))
        ang = 2.0 * np.pi * np.outer(np.arange(rows), np.arange(GRID_W)) / seq_len
        ang = ang.reshape(rows, GRID_W // SUBLANES, SUBLANES).transpose(1, 0, 2).reshape(-1, rows * SUBLANES)
        tw_cos = jnp.asarray(np.broadcast_to(np.cos(ang)[:, :, None], ang.shape + (FNET_GW,)), F32)
        tw_sin = jnp.asarray(np.broadcast_to(np.sin(ang)[:, :, None], ang.shape + (FNET_GW,)), F32)
        p = _fft_rows(uv, row_mat, tw_cos, tw_sin, batch, rows)
        return _fft_cols(p, col_mat, batch, rows, scale)
    c_l, s_l = _dft_cos_sin(seq_len)
    return _fnet_dense(uv, _mxu_const(c_l), _mxu_const(s_l), batch, seq_len, scale)


def _conv_kernel(tiles_per_seq, tl, scb_ref, scc_ref, scx_ref, cfa_ref, cfb_ref,
                 scc_p, scx_p, cfa_p, cfb_p, scc_n, scx_n, cfa_n, cfb_n,
                 wsc_ref, wcf_ref, ln_ref, osc_ref, ocf_ref, esc_ref, ecf_ref):
    i = pl.program_id(0)
    keep_prev = jnp.where(i % tiles_per_seq == 0, 0.0, 1.0)
    keep_next = jnp.where(i % tiles_per_seq == tiles_per_seq - 1, 0.0, 1.0)
    ext = tl + 2 * HALO_ROWS

    def gated(a_ref, b_ref):
        return a_ref[...].astype(F32) * _sigmoid(b_ref[...].astype(F32))

    def prod(a_ref, b_ref):
        return a_ref[...].astype(F32) * b_ref[...].astype(F32)

    esc_ref[0:HALO_ROWS] = prod(scc_p, scx_p) * keep_prev
    esc_ref[HALO_ROWS:HALO_ROWS + tl] = prod(scc_ref, scx_ref)
    esc_ref[HALO_ROWS + tl:] = prod(scc_n, scx_n) * keep_next
    ecf_ref[0, 0:HALO_ROWS] = gated(cfa_p, cfb_p) * keep_prev
    ecf_ref[0, HALO_ROWS:HALO_ROWS + tl] = gated(cfa_ref, cfb_ref)
    ecf_ref[0, HALO_ROWS + tl:ext] = gated(cfa_n, cfb_n) * keep_next
    for s in range(1, SUBLANES):
        ecf_ref[s, 0:ext - SUBLANES] = ecf_ref[0, s:ext - SUBLANES + s]

    rs = 32
    lane_groups = BRANCH_W // LANES
    for r in range(tl // rs):
        r0 = r * rs
        accs = []
        for lg in range(lane_groups):
            ls = slice(lg * LANES, (lg + 1) * LANES)
            acc = None
            for j in range(SC_WIDTH):
                off = HALO_ROWS + r0 + j - (SC_WIDTH - 1) // 2
                tap = wsc_ref[j:j + 1, ls] * esc_ref[off:off + rs, ls]
                acc = tap if acc is None else acc + tap
            osc_ref[r0:r0 + rs, ls] = (scb_ref[r0:r0 + rs, ls].astype(F32) * acc).astype(osc_ref.dtype)
            acc = None
            for j in range(CF_WIDTH):
                off = HALO_ROWS + r0 + j - (CF_WIDTH - 1) // 2
                s = off % SUBLANES
                tap = wcf_ref[j:j + 1, ls] * ecf_ref[s, off - s:off - s + rs, ls]
                acc = tap if acc is None else acc + tap
            accs.append(acc)
        tot = accs[0]
        for a in accs[1:]:
            tot = tot + a
        mu = jnp.sum(tot, axis=-1, keepdims=True) * (1.0 / BRANCH_W)
        sq = None
        for a in accs:
            d = a - mu
            sq = d * d if sq is None else sq + d * d
        inv = lax.rsqrt(jnp.sum(sq, axis=-1, keepdims=True) * (1.0 / BRANCH_W) + EPS)
        for lg, a in enumerate(accs):
            ls = slice(lg * LANES, (lg + 1) * LANES)
            z = (a - mu) * inv * ln_ref[0:1, ls] + ln_ref[1:2, ls]
            ocf_ref[r0:r0 + rs, ls] = (z * _sigmoid(z)).astype(ocf_ref.dtype)


def _conv_branches(parts, sc_conv, cf_conv, cf_ln, layer, seq_len):
    t = parts.shape[0]
    tl = min(256, seq_len)
    assert seq_len % tl == 0 and tl % HALO_ROWS == 0
    tiles_per_seq = seq_len // tl
    halo_per_tile = tl // HALO_ROWS
    n_halo = t // HALO_ROWS
    main = lambda col: pl.BlockSpec((tl, BRANCH_W), lambda i: (i, col))
    prev = lambda col: pl.BlockSpec((HALO_ROWS, BRANCH_W), lambda i: (jnp.maximum(i * halo_per_tile - 1, 0), col))
    nxt = lambda col: pl.BlockSpec((HALO_ROWS, BRANCH_W),
                                   lambda i: (jnp.minimum((i + 1) * halo_per_tile, n_halo - 1), col))
    per_layer = lambda a: pl.BlockSpec((None,) + a.shape[1:], lambda i: (layer, 0, 0))
    out = pl.BlockSpec((tl, BRANCH_W), lambda i: (i, 0))
    ext = tl + 2 * HALO_ROWS
    return pl.pallas_call(
        functools.partial(_conv_kernel, tiles_per_seq, tl),
        grid=(t // tl,),
        in_specs=[main(5), main(6), main(7), main(8), main(9),
                  prev(6), prev(7), prev(8), prev(9), nxt(6), nxt(7), nxt(8), nxt(9),
                  per_layer(sc_conv), per_layer(cf_conv), per_layer(cf_ln)],
        out_specs=[out, out],
        out_shape=[jax.ShapeDtypeStruct((t, BRANCH_W), BF16)] * 2,
        scratch_shapes=[pltpu.VMEM((ext, BRANCH_W), F32), pltpu.VMEM((SUBLANES, ext, BRANCH_W), F32)],
        compiler_params=_params("arbitrary"),
        name="conv_branches",
    )(*([parts] * 13), sc_conv, cf_conv, cf_ln)


MIX_COL_TILE = 512
MIX_GATE_BLOCK = 1024


def _mix_kernel(cast_blocks, *refs):
    y_refs = refs[:N_BRANCH]
    gates_per_branch = D_MODEL // MIX_GATE_BLOCK
    n_gates = N_BRANCH * gates_per_branch
    n_cast = len(cast_blocks)
    g_refs = refs[N_BRANCH:N_BRANCH + n_gates]
    wb_ref, wo_ref, x_ref, ng_ref, mod_ref = refs[N_BRANCH + n_gates:N_BRANCH + n_gates + 5]
    rest = refs[N_BRANCH + n_gates + 5:]
    cast_src, (xo_ref, h_ref), cast_dst, m_ref = rest[:n_cast], rest[n_cast:n_cast + 2], rest[n_cast + 2:-1], rest[-1]
    _cast_step(pl.program_id(0), cast_blocks, cast_src, cast_dst)
    ys = [y_ref[...].astype(BF16) for y_ref in y_refs]
    for j in range(D_MODEL // MIX_COL_TILE):
        c0 = j * MIX_COL_TILE
        cs = slice(c0, c0 + MIX_COL_TILE)
        merged = None
        for n in range(N_BRANCH):
            p = jnp.dot(ys[n], wb_ref[n, :, cs], preferred_element_type=F32)
            g_ref = g_refs[n * gates_per_branch + c0 // MIX_GATE_BLOCK]
            g0 = c0 % MIX_GATE_BLOCK
            term = g_ref[:, g0:g0 + MIX_COL_TILE].astype(F32) * p
            merged = term if merged is None else merged + term
        m_ref[:, cs] = merged.astype(BF16)
    mix = jnp.dot(m_ref[...], wo_ref[...], preferred_element_type=F32)
    x1 = x_ref[...] + mod_ref[2:3, :] * (_rms(mix) * ng_ref[1:2, :])
    xo_ref[...] = x1
    h = _rms(x1) * ng_ref[2:3, :]
    h_ref[...] = (h * (1.0 + mod_ref[4:5, :]) + mod_ref[3:4, :]).astype(h_ref.dtype)


def _mix(ys, parts, w_branch, w_out, x, norm_g, mod, layer, mod_span, cast_mats=()):
    t, d = x.shape
    tm = min(256, mod_span)
    per_seq = mod_span // tm
    gate0 = GATE_COL0 // MIX_GATE_BLOCK
    n_gate_blocks = N_BRANCH * d // MIX_GATE_BLOCK
    resident = pl.Buffered(1)
    y_spec = pl.BlockSpec((tm, BRANCH_W), lambda i: (i, 0))
    gate = lambda k: pl.BlockSpec((tm, MIX_GATE_BLOCK), lambda i: (i, gate0 + k))
    row_tile = pl.BlockSpec((tm, d), lambda i: (i, 0))
    jobs = _cast_jobs(cast_mats, t // tm)
    cast_specs, cast_shapes = _cast_specs(jobs, lambda i: i)
    out = pl.pallas_call(
        functools.partial(_mix_kernel, tuple(nb for _, _, nb in jobs)),
        grid=(t // tm,),
        in_specs=[y_spec] * N_BRANCH + [gate(k) for k in range(n_gate_blocks)] + [
            pl.BlockSpec((None, N_BRANCH, BRANCH_W, d), lambda i: (layer, 0, 0, 0), pipeline_mode=resident),
            pl.BlockSpec((None, d, d), lambda i: (layer, 0, 0), pipeline_mode=resident),
            row_tile,
            pl.BlockSpec((None, 4, d), lambda i: (layer, 0, 0)),
            pl.BlockSpec((None, 6, d), lambda i: (i // per_seq, 0, 0))] + cast_specs,
        out_specs=[row_tile, row_tile] + cast_specs,
        out_shape=[jax.ShapeDtypeStruct((t, d), F32), jax.ShapeDtypeStruct((t, d), BF16)] + cast_shapes,
        scratch_shapes=[pltpu.VMEM((tm, d), BF16)],
        compiler_params=_params("arbitrary"),
        name="mix",
    )(*ys, *([parts] * n_gate_blocks), w_branch, w_out, x, norm_g, mod, *[a for a, _, _ in jobs])
    return out[0], out[1], out[2:]


def _ffn_kernel(n_f, cast_blocks, h_ref, w1_ref, w2_ref, x_ref, ng_ref, mod_ref, *rest):
    n_cast = len(cast_blocks)
    cast_src, o_ref, cast_dst, acc_ref = rest[:n_cast], rest[n_cast], rest[n_cast + 1:-1], rest[-1]
    f = pl.program_id(1)
    _cast_step(pl.program_id(0) * n_f + f, cast_blocks, cast_src, cast_dst)

    def partial_sum():
        a = jnp.maximum(jnp.dot(h_ref[...], w1_ref[...], preferred_element_type=F32), 0.0)
        return jnp.dot((a * a).astype(BF16), w2_ref[...], preferred_element_type=F32)

    @pl.when(f == 0)
    def _():
        acc_ref[...] = partial_sum()

    @pl.when(f != 0)
    def _():
        acc_ref[...] += partial_sum()

    @pl.when(f == n_f - 1)
    def _():
        o_ref[...] = x_ref[...] + mod_ref[5:6, :] * (_rms(acc_ref[...]) * ng_ref[3:4, :])


def _ffn(h, w1, w2, x, norm_g, mod, layer, mod_span, cast_mats=()):
    t, d = x.shape
    d_ff = w1.shape[1]
    tm = min(512, mod_span)
    tf = 1024
    per_seq = mod_span // tm
    n_i, n_f = t // tm, d_ff // tf
    row_tile = pl.BlockSpec((tm, d), lambda i, f: (i, 0))
    jobs = _cast_jobs(cast_mats, n_i * n_f)
    cast_specs, cast_shapes = _cast_specs(jobs, lambda i, f: i * n_f + f)
    out = pl.pallas_call(
        functools.partial(_ffn_kernel, n_f, tuple(nb for _, _, nb in jobs)),
        grid=(n_i, n_f),
        in_specs=[
            row_tile,
            pl.BlockSpec((d, tf), lambda i, f: (0, f)),
            pl.BlockSpec((tf, d), lambda i, f: (f, 0)),
            row_tile,
            pl.BlockSpec((None, 4, d), lambda i, f: (layer, 0, 0)),
            pl.BlockSpec((None, 6, d), lambda i, f: (i // per_seq, 0, 0)),
        ] + cast_specs,
        out_specs=[row_tile] + cast_specs,
        out_shape=[jax.ShapeDtypeStruct((t, d), F32)] + cast_shapes,
        scratch_shapes=[pltpu.VMEM((tm, d), F32)],
        compiler_params=_params("arbitrary", "arbitrary"),
        name="ffn",
    )(h, w1, w2, x, norm_g, mod, *[a for a, _, _ in jobs])
    return out[0], out[1:]


def _rope_tables(seq_len):
    rows = seq_len // GRID_W
    row = jnp.repeat(jnp.arange(rows, dtype=F32), GRID_W)
    col = jnp.tile(jnp.arange(GRID_W, dtype=F32), rows)
    freqs = ROPE_BASE ** (-jnp.arange(ROPE_PAIRS, dtype=F32) / ROPE_PAIRS)
    ang = jnp.concatenate([row[:, None] * freqs[None], col[:, None] * freqs[None]], axis=-1)
    cos, sin = jnp.cos(ang), jnp.sin(ang)
    return jnp.concatenate([cos, cos], axis=-1), jnp.concatenate([-sin, sin], axis=-1)


def kernel(x, c, ctx, c_ctx, w_ada, b_ada, norm_g, w_in, ret_decay, sc_conv, cf_conv, cf_ln,
           w_branch, w_out, w_ff1, w_ff2):
    batch, seq_len, d = x.shape
    ctx_len = ctx.shape[1]
    depth = w_ada.shape[0]
    in_cols = w_in.shape[2]
    assert d == D_MODEL and seq_len % (GRID_W * 16) == 0 and ctx_len % RET_CHUNK == 0

    n_rows = -(-(batch + 1) // BF16_ROWS) * BF16_ROWS
    c_rows = jnp.concatenate([c, c_ctx[None], jnp.zeros((n_rows - batch - 1, d), F32)], axis=0)
    mod_all = _ada(c_rows, w_ada, b_ada).reshape(depth, n_rows, 6, d)

    rope_tabs = _rope_tables(seq_len)
    xf = x.reshape(batch * seq_len, d)
    xc = ctx.reshape(batch * ctx_len, d)
    zeros_state = jnp.zeros((batch, RET_HEADS, RET_DK, RET_DK), F32)
    wb_b, wo_b = w_branch.astype(BF16), w_out.astype(BF16)
    w_in_l = w_in[0].astype(BF16)
    w1_l = w2_l = None

    def mixer_tail(parts, y_ret, x_res, mod, layer, length, mod_span, w1, w2, mix_casts=(), ffn_casts=()):
        y_f = _fourier_mix(parts, batch, length)
        y_sc, y_cf = _conv_branches(parts, sc_conv, cf_conv, cf_ln, layer, length)
        x1, h2, mix_cast = _mix((y_ret, y_f, y_sc, y_cf), parts, wb_b, wo_b, x_res, norm_g, mod, layer, mod_span,
                                mix_casts)
        x2, ffn_cast = _ffn(h2, w1, w2, x1, norm_g, mod, layer, mod_span, ffn_casts)
        return x2, mix_cast, ffn_cast

    for layer in range(depth):
        with_ctx = layer < depth - 1
        mod = mod_all[layer, :batch]
        mod_c = mod_all[layer, batch:batch + 1]
        tables = _ret_tables(ret_decay[layer].astype(F32))

        ctx_cols = in_cols if with_ctx else 2 * BRANCH_W
        parts_c, _ = _inproj(xc, norm_g, mod_c, w_in_l, layer, ctx_cols, batch * ctx_len)
        stc_f, stc_b, s0f, s0b = _ret_state(parts_c, tables, zeros_state, zeros_state, None, batch, ctx_len)

        first_casts = (w_ff1[0], w_ff2[0]) if layer == 0 else ()
        parts, cast = _inproj(xf, norm_g, mod, w_in_l, layer, in_cols, seq_len, first_casts)
        if layer == 0:
            w1_l, w2_l = cast
        st_f, st_b, _, _ = _ret_state(parts, tables, s0f, s0b, rope_tabs, batch, seq_len)
        y_ret = _ret_out(parts, tables, st_f, st_b, rope_tabs, batch, seq_len)
        mix_casts = (w_in[layer + 1],) if with_ctx else ()
        ffn_casts = (w_ff1[layer + 1], w_ff2[layer + 1]) if with_ctx else ()
        xf, mix_cast, ffn_cast = mixer_tail(parts, y_ret, xf, mod, layer, seq_len, seq_len, w1_l, w2_l,
                                            mix_casts, ffn_casts)

        if with_ctx:
            y_ret_c = _ret_out(parts_c, tables, stc_f, stc_b, None, batch, ctx_len)
            xc, _, _ = mixer_tail(parts_c, y_ret_c, xc, mod_c, layer, ctx_len, batch * ctx_len, w1_l, w2_l)
            (w_in_l,), (w1_l, w2_l) = mix_cast, ffn_cast
    return xf.reshape(batch, seq_len, d)
```

```python
import functools
import math

import numpy as np
import jax
import jax.numpy as jnp
from jax import lax
from jax.experimental import pallas as pl
from jax.experimental.pallas import tpu as pltpu

F32 = jnp.float32
BF16 = jnp.bfloat16

D_MODEL = 2048
GRID_W = 64
N_BRANCH = 4
BRANCH_W = D_MODEL // 4
RET_HEADS = 4
RET_DK = BRANCH_W // RET_HEADS
RET_CHUNK = 128
ROPE_PAIRS = RET_DK // 4
ROPE_BASE = 10000.0
FNET_GROUPS = 4
FNET_GW = BRANCH_W // FNET_GROUPS
SC_WIDTH = 3
CF_WIDTH = 31
GATE_COL0 = 10 * BRANCH_W
EPS = 1e-6
K_SCALE = RET_DK ** -0.5

VMEM_LIMIT_BYTES = 56 * 1024 * 1024
SUBLANES = 8
BF16_ROWS = 16
HALO_ROWS = BF16_ROWS
LANES = 128


def _params(*sem):
    return pltpu.CompilerParams(dimension_semantics=sem, vmem_limit_bytes=VMEM_LIMIT_BYTES)


def _sigmoid(v):
    return 0.5 * jnp.tanh(0.5 * v) + 0.5


def _log_sigmoid(v):
    return jnp.minimum(v, 0.0) - jnp.log1p(jnp.exp(-jnp.abs(v)))


def _rms(v):
    return v * lax.rsqrt(jnp.mean(v * v, axis=-1, keepdims=True) + EPS)


def _ada_kernel(c_ref, w_ref, b_ref, o_ref):
    c = c_ref[...]
    s = c * _sigmoid(c)
    w = w_ref[...]
    s_hi = s.astype(BF16)
    s_lo = (s - s_hi.astype(F32)).astype(BF16)
    w_hi = w.astype(BF16)
    w_lo = (w - w_hi.astype(F32)).astype(BF16)
    acc = jnp.dot(s_hi, w_hi, preferred_element_type=F32)
    acc = acc + jnp.dot(s_hi, w_lo, preferred_element_type=F32)
    acc = acc + jnp.dot(s_lo, w_hi, preferred_element_type=F32)
    o_ref[...] = acc + b_ref[...]


def _ada(c_rows, w_ada, b_ada):
    depth, d, n = w_ada.shape
    rows = c_rows.shape[0]
    tn = 1024
    return pl.pallas_call(
        _ada_kernel,
        grid=(depth, n // tn),
        in_specs=[
            pl.BlockSpec((rows, d), lambda l, j: (0, 0)),
            pl.BlockSpec((None, d, tn), lambda l, j: (l, 0, j)),
            pl.BlockSpec((None, 1, tn), lambda l, j: (l, 0, j)),
        ],
        out_specs=pl.BlockSpec((None, rows, tn), lambda l, j: (l, 0, j)),
        out_shape=jax.ShapeDtypeStruct((depth, rows, n), F32),
        compiler_params=_params("arbitrary", "arbitrary"),
        name="ada",
    )(c_rows, w_ada, b_ada.reshape(depth, 1, n))


def _cast_jobs(mats, n_steps):
    jobs = []
    for a, layer in mats:
        rows = a.shape[1]
        rb = BF16_ROWS
        while rows // rb > n_steps:
            rb *= 2
        assert rows % rb == 0
        jobs.append((a, layer, rb, rows // rb))
    return jobs


def _cast_specs(jobs, step_of):
    blk = lambda nb: (lambda *g: jnp.minimum(step_of(*g), nb - 1))
    src = [pl.BlockSpec((None, rb, a.shape[2]), lambda *g, b=blk(nb), l=layer: (l, b(*g), 0))
           for a, layer, rb, nb in jobs]
    dst = [pl.BlockSpec((rb, a.shape[2]), lambda *g, b=blk(nb): (b(*g), 0)) for a, _, rb, nb in jobs]
    return src, dst, [jax.ShapeDtypeStruct(a.shape[1:], BF16) for a, _, _, _ in jobs]


def _cast_step(step, jobs_blocks, src_refs, dst_refs):
    for nb, src, dst in zip(jobs_blocks, src_refs, dst_refs):
        @pl.when(step < nb)
        def _(src=src, dst=dst):
            dst[...] = src[...].astype(BF16)


FNET_COL0 = 4 * BRANCH_W


def _inproj_kernel(gate_tile0, fnet_tile, fnet_lane0, n_j, cast_blocks, x_ref, g_ref, mod_ref, w_ref, *rest):
    n_cast = len(cast_blocks)
    n_out = 1 if fnet_tile is None else 2
    cast_src, outs = rest[:n_cast], rest[n_cast:n_cast + n_out]
    cast_dst, h_ref = rest[n_cast + n_out:-1], rest[-1]
    o_ref = outs[0]
    j = pl.program_id(1)
    _cast_step(pl.program_id(0) * n_j + j, cast_blocks, cast_src, cast_dst)

    @pl.when(j == 0)
    def _():
        y = _rms(x_ref[...]) * g_ref[0:1, :]
        h = y * (1.0 + mod_ref[1:2, :]) + mod_ref[0:1, :]
        h_ref[...] = h.astype(BF16)

    def project():
        return jnp.dot(h_ref[...], w_ref[...], preferred_element_type=F32)

    plain = j < gate_tile0
    if fnet_tile is not None:
        plain = jnp.logical_and(plain, j != fnet_tile)

        @pl.when(j == fnet_tile)
        def _():
            r = project()
            o_ref[...] = r.astype(o_ref.dtype)
            outs[1][...] = r[:, fnet_lane0:fnet_lane0 + BRANCH_W]

    @pl.when(plain)
    def _():
        o_ref[...] = project().astype(o_ref.dtype)

    @pl.when(j >= gate_tile0)
    def _():
        o_ref[...] = _sigmoid(project()).astype(o_ref.dtype)


def _inproj(x, norm_g, mod, w, layer, n_cols, mod_span, cast_mats=(), emit_fnet=False):
    t, d = x.shape
    tm = min(1024, mod_span)
    tn = min(1024, n_cols)
    assert GATE_COL0 % tn == 0 or n_cols <= GATE_COL0
    per_seq = mod_span // tm
    n_i, n_j = t // tm, n_cols // tn
    jobs = _cast_jobs(cast_mats, n_i * n_j)
    cast_src, cast_dst, cast_shapes = _cast_specs(jobs, lambda i, j: i * n_j + j)
    fnet_tile = fnet_lane0 = None
    fnet_spec, fnet_shape = [], []
    if emit_fnet:
        fnet_tile, fnet_lane0 = FNET_COL0 // tn, FNET_COL0 % tn
        assert fnet_lane0 + BRANCH_W <= tn and fnet_tile < GATE_COL0 // tn
        fnet_spec = [pl.BlockSpec((tm, BRANCH_W), lambda i, j: (i, 0))]
        fnet_shape = [jax.ShapeDtypeStruct((t, BRANCH_W), F32)]
    out = pl.pallas_call(
        functools.partial(_inproj_kernel, GATE_COL0 // tn, fnet_tile, fnet_lane0, n_j,
                          tuple(job[-1] for job in jobs)),
        grid=(n_i, n_j),
        in_specs=[
            pl.BlockSpec((tm, d), lambda i, j: (i, 0)),
            pl.BlockSpec((None, 4, d), lambda i, j: (layer, 0, 0)),
            pl.BlockSpec((None, 6, d), lambda i, j: (i // per_seq, 0, 0)),
            pl.BlockSpec((d, tn), lambda i, j: (0, j)),
        ] + cast_src,
        out_specs=[pl.BlockSpec((tm, tn), lambda i, j: (i, j))] + fnet_spec + cast_dst,
        out_shape=[jax.ShapeDtypeStruct((t, n_cols), BF16)] + fnet_shape + cast_shapes,
        scratch_shapes=[pltpu.VMEM((tm, d), BF16)],
        compiler_params=_params("arbitrary", "arbitrary"),
        name="inproj",
    )(x, norm_g, mod, w, *[job[0] for job in jobs])
    n_out = 1 + len(fnet_shape)
    return out[0], (out[1] if emit_fnet else None), out[n_out:]


def _ret_tables_kernel(dec_ref, dmat_ref, qdec_ref, kdec_ref, cdec_ref):
    row = lax.broadcasted_iota(jnp.int32, (RET_CHUNK, RET_CHUNK), 0).astype(F32)
    col = lax.broadcasted_iota(jnp.int32, (RET_CHUNK, RET_CHUNK), 1).astype(F32)
    for h in range(RET_HEADS):
        lg_f = _log_sigmoid(jnp.full((1, RET_DK), dec_ref[0, h], F32))
        lg_b = _log_sigmoid(jnp.full((1, RET_DK), dec_ref[1, h], F32))
        dmat_ref[h] = K_SCALE * jnp.where(row >= col, jnp.exp(lg_f * jnp.maximum(row - col, 0.0)),
                                          jnp.exp(lg_b * jnp.maximum(col - row, 0.0)))
        qdec_ref[0, h] = jnp.exp(lg_f * (row + 1.0))
        qdec_ref[1, h] = jnp.exp(lg_b * (RET_CHUNK - row))
        kdec_ref[0, h] = K_SCALE * jnp.exp(lg_f * (RET_CHUNK - 1.0 - row))
        kdec_ref[1, h] = K_SCALE * jnp.exp(lg_b * row)
        cdec_ref[0, h] = jnp.broadcast_to(jnp.exp(lg_f * RET_CHUNK), (SUBLANES, RET_DK))
        cdec_ref[1, h] = jnp.broadcast_to(jnp.exp(lg_b * RET_CHUNK), (SUBLANES, RET_DK))


def _ret_tables(dec):
    sq = (RET_HEADS, RET_CHUNK, RET_CHUNK)
    return pl.pallas_call(
        _ret_tables_kernel,
        in_specs=[pl.BlockSpec(memory_space=pltpu.SMEM)],
        out_shape=[jax.ShapeDtypeStruct(sq, F32), jax.ShapeDtypeStruct((2,) + sq, F32),
                   jax.ShapeDtypeStruct((2,) + sq, F32),
                   jax.ShapeDtypeStruct((2, RET_HEADS, SUBLANES, RET_DK), F32)],
        name="ret_tables",
    )(dec)


def _rope(t, cos, sin):
    return t * cos + pltpu.roll(t, RET_DK // 2, 1) * sin


def _ret_state_kernel(use_rope, cps, n_steps, *refs):
    if use_rope:
        (kdec_ref, cdec_ref, kf_ref, vf_ref, kb_ref, vb_ref, cf_ref, sf_ref, cb_ref, sb_ref, s0f_ref, s0b_ref,
         of_ref, ob_ref, ff_ref, fb_ref, st_ref) = refs
    else:
        (kdec_ref, cdec_ref, kf_ref, vf_ref, kb_ref, vb_ref, s0f_ref, s0b_ref,
         of_ref, ob_ref, ff_ref, fb_ref, st_ref) = refs
    i = pl.program_id(1)

    @pl.when(i == 0)
    def _():
        st_ref[0] = s0f_ref[...]
        st_ref[1] = s0b_ref[...]

    for cc in range(cps):
        rf = slice(cc * RET_CHUNK, (cc + 1) * RET_CHUNK)
        rb = slice((cps - 1 - cc) * RET_CHUNK, (cps - cc) * RET_CHUNK)
        for h in range(RET_HEADS):
            hs = slice(h * RET_DK, (h + 1) * RET_DK)
            kf = kf_ref[rf, hs].astype(F32)
            kb = kb_ref[rb, hs].astype(F32)
            if use_rope:
                kf = _rope(kf, cf_ref[rf, :], sf_ref[rf, :])
                kb = _rope(kb, cb_ref[rb, :], sb_ref[rb, :])
            kf = kf * kdec_ref[0, h]
            kb = kb * kdec_ref[1, h]
            upd_f = jnp.dot(kf.T.astype(BF16), vf_ref[rf, hs], preferred_element_type=F32)
            upd_b = jnp.dot(kb.T.astype(BF16), vb_ref[rb, hs], preferred_element_type=F32)
            s_f = st_ref[0, h]
            s_b = st_ref[1, h]
            of_ref[cc, h] = s_f.astype(of_ref.dtype)
            ob_ref[cps - 1 - cc, h] = s_b.astype(ob_ref.dtype)
            st_ref[0, h] = s_f * cdec_ref[0, h, 0:1, :] + upd_f
            st_ref[1, h] = s_b * cdec_ref[1, h, 0:1, :] + upd_b

    @pl.when(i == n_steps - 1)
    def _():
        ff_ref[...] = st_ref[0]
        fb_ref[...] = st_ref[1]


def _chunks_per_step(n_chunks):
    return math.gcd(n_chunks, 8)


def _ret_state(parts, tables, s0f, s0b, rope_tabs, batch, seq_len):
    n = seq_len // RET_CHUNK
    cps = _chunks_per_step(n)
    ns = n // cps
    rows = cps * RET_CHUNK
    use_rope = rope_tabs is not None
    _, _, kdec, cdec = tables
    blk = lambda col, rev: pl.BlockSpec(
        (rows, BRANCH_W), (lambda b, i: (b * ns + ns - 1 - i, col)) if rev else (lambda b, i: (b * ns + i, col)))
    tab = lambda rev: pl.BlockSpec((rows, RET_DK), (lambda b, i: (ns - 1 - i, 0)) if rev else (lambda b, i: (i, 0)))
    st_in = pl.BlockSpec((None, RET_HEADS, RET_DK, RET_DK), lambda b, i: (b, 0, 0, 0))
    full = lambda a: pl.BlockSpec(a.shape, lambda b, i: (0,) * a.ndim)
    in_specs = [full(kdec), full(cdec), blk(0, False), blk(1, False), blk(0, True), blk(1, True)]
    args = [kdec, cdec, parts, parts, parts, parts]
    if use_rope:
        in_specs += [tab(False), tab(False), tab(True), tab(True)]
        args += [rope_tabs[0], rope_tabs[1], rope_tabs[0], rope_tabs[1]]
    in_specs += [st_in, st_in]
    args += [s0f, s0b]
    st_shape = jax.ShapeDtypeStruct((batch, n, RET_HEADS, RET_DK, RET_DK), BF16)
    fin_shape = jax.ShapeDtypeStruct((batch, RET_HEADS, RET_DK, RET_DK), F32)
    return pl.pallas_call(
        functools.partial(_ret_state_kernel, use_rope, cps, ns),
        grid=(batch, ns),
        in_specs=in_specs,
        out_specs=[
            pl.BlockSpec((None, cps, RET_HEADS, RET_DK, RET_DK), lambda b, i: (b, i, 0, 0, 0)),
            pl.BlockSpec((None, cps, RET_HEADS, RET_DK, RET_DK), lambda b, i: (b, ns - 1 - i, 0, 0, 0)),
            st_in, st_in,
        ],
        out_shape=[st_shape, st_shape, fin_shape, fin_shape],
        scratch_shapes=[pltpu.VMEM((2, RET_HEADS, RET_DK, RET_DK), F32)],
        compiler_params=_params("arbitrary", "arbitrary"),
        name="ret_state",
    )(*args)


def _ret_out_kernel(use_rope, cps, *refs):
    if use_rope:
        dmat_ref, qdec_ref, k_ref, v_ref, q_ref, g_ref, cos_ref, sin_ref, sf_ref, sb_ref, o_ref = refs
    else:
        dmat_ref, qdec_ref, k_ref, v_ref, q_ref, g_ref, sf_ref, sb_ref, o_ref = refs
    for cc in range(cps):
        rs = slice(cc * RET_CHUNK, (cc + 1) * RET_CHUNK)
        for h in range(RET_HEADS):
            hs = slice(h * RET_DK, (h + 1) * RET_DK)
            q = q_ref[rs, hs].astype(F32)
            k = k_ref[rs, hs].astype(F32)
            if use_rope:
                cos, sin = cos_ref[rs, :], sin_ref[rs, :]
                q = _rope(q, cos, sin)
                k = _rope(k, cos, sin)
            scores = lax.dot_general(q.astype(BF16), k.astype(BF16), (((1,), (1,)), ((), ())),
                                     preferred_element_type=F32)
            y = jnp.dot((scores * dmat_ref[h]).astype(BF16), v_ref[rs, hs], preferred_element_type=F32)
            y = y + jnp.dot((q * qdec_ref[0, h]).astype(BF16), sf_ref[cc, h], preferred_element_type=F32)
            y = y + jnp.dot((q * qdec_ref[1, h]).astype(BF16), sb_ref[cc, h], preferred_element_type=F32)
            mu = jnp.mean(y, axis=-1, keepdims=True)
            yc = y - mu
            yn = yc * lax.rsqrt(jnp.mean(yc * yc, axis=-1, keepdims=True) + EPS)
            g = g_ref[rs, hs].astype(F32)
            o_ref[rs, hs] = (yn * (g * _sigmoid(g))).astype(o_ref.dtype)


def _ret_out(parts, tables, st_f, st_b, rope_tabs, batch, seq_len):
    n = seq_len // RET_CHUNK
    cps = _chunks_per_step(n)
    ns = n // cps
    rows = cps * RET_CHUNK
    use_rope = rope_tabs is not None
    dmat, qdec, _, _ = tables
    blk = lambda col: pl.BlockSpec((rows, BRANCH_W), lambda b, i: (b * ns + i, col))
    st = pl.BlockSpec((None, cps, RET_HEADS, RET_DK, RET_DK), lambda b, i: (b, i, 0, 0, 0))
    full = lambda a: pl.BlockSpec(a.shape, lambda b, i: (0,) * a.ndim)
    in_specs = [full(dmat), full(qdec), blk(0), blk(1), blk(2), blk(3)]
    args = [dmat, qdec, parts, parts, parts, parts]
    if use_rope:
        tab = pl.BlockSpec((rows, RET_DK), lambda b, i: (i, 0))
        in_specs += [tab, tab]
        args += list(rope_tabs)
    in_specs += [st, st]
    args += [st_f, st_b]
    return pl.pallas_call(
        functools.partial(_ret_out_kernel, use_rope, cps),
        grid=(batch, ns),
        in_specs=in_specs,
        out_specs=blk(0),
        out_shape=jax.ShapeDtypeStruct((batch * seq_len, BRANCH_W), BF16),
        compiler_params=_params("arbitrary", "arbitrary"),
        name="ret_out",
    )(*args)


def _chan_dft_kernel(x_ref, m_ref, o_ref):
    for g in range(FNET_GROUPS):
        r = jnp.dot(x_ref[:, g * FNET_GW:(g + 1) * FNET_GW], m_ref[...], preferred_element_type=F32)
        o_ref[:, g * FNET_GW:(g + 1) * FNET_GW] = r[:, :FNET_GW]
        o_ref[:, BRANCH_W + g * FNET_GW:BRANCH_W + (g + 1) * FNET_GW] = r[:, FNET_GW:]


def _chan_dft(parts, chan_mat):
    t = parts.shape[0]
    tm = min(1024, t)
    return pl.pallas_call(
        _chan_dft_kernel,
        grid=(t // tm,),
        in_specs=[pl.BlockSpec((tm, BRANCH_W), lambda i: (i, 4)),
                  pl.BlockSpec((FNET_GW, 2 * FNET_GW), lambda i: (0, 0))],
        out_specs=pl.BlockSpec((tm, 2 * BRANCH_W), lambda i: (i, 0)),
        out_shape=jax.ShapeDtypeStruct((t, 2 * BRANCH_W), F32),
        compiler_params=_params("arbitrary"),
        name="chan_dft",
    )(parts, chan_mat)


def _fft_rows_kernel(rows, cols_per_step, x_ref, f_ref, tc_ref, ts_ref, o_ref):
    n = rows * cols_per_step
    x2d = x_ref[...].reshape(n, BRANCH_W).astype(BF16)
    res = jnp.dot(f_ref[...], x2d, preferred_element_type=F32)
    tc = tc_ref[...]
    ts = ts_ref[...]
    for g in range(FNET_GROUPS):
        gs = slice(g * FNET_GW, (g + 1) * FNET_GW)
        cx = res[:n, gs]
        sx = res[n:, gs]
        o_ref[0, :, :, gs] = (cx * tc - sx * ts).reshape(rows, cols_per_step, FNET_GW)
        o_ref[1, :, :, gs] = (-(sx * tc + cx * ts)).reshape(rows, cols_per_step, FNET_GW)


def _fft_rows(u, f_mat, tw_cos, tw_sin, batch, rows):
    cols_per_step = SUBLANES
    n = rows * cols_per_step
    x = u.reshape(batch, rows, GRID_W, BRANCH_W)
    return pl.pallas_call(
        functools.partial(_fft_rows_kernel, rows, cols_per_step),
        grid=(batch, GRID_W // cols_per_step),
        in_specs=[
            pl.BlockSpec((None, rows, cols_per_step, BRANCH_W), lambda b, j: (b, 0, j, 0)),
            pl.BlockSpec((2 * n, n), lambda b, j: (0, 0)),
            pl.BlockSpec((None, n, FNET_GW), lambda b, j: (j, 0, 0)),
            pl.BlockSpec((None, n, FNET_GW), lambda b, j: (j, 0, 0)),
        ],
        out_specs=pl.BlockSpec((None, 2, rows, cols_per_step, BRANCH_W), lambda b, j: (b, 0, 0, j, 0)),
        out_shape=jax.ShapeDtypeStruct((batch, 2, rows, GRID_W, BRANCH_W), F32),
        compiler_params=_params("arbitrary", "arbitrary"),
        name="fft_rows",
    )(x, f_mat, tw_cos, tw_sin)


def _fft_cols_kernel(scale, k1_per_step, p_ref, f_ref, cs_ref, o_ref):
    z_re, z_im = [], []
    for k1 in range(k1_per_step):
        stacked = jnp.concatenate([p_ref[0, k1], p_ref[1, k1]], axis=0).astype(BF16)
        z = jnp.dot(f_ref[...], stacked, preferred_element_type=F32)
        z_re.append(z[:GRID_W])
        z_im.append(z[GRID_W:])
    z_re = jnp.concatenate(z_re, axis=0)
    z_im = jnp.concatenate(z_im, axis=0)
    for g in range(FNET_GROUPS):
        gs = slice(g * FNET_GW, (g + 1) * FNET_GW)
        lhs = jnp.concatenate([z_re[:, gs], z_im[:, gs]], axis=1).astype(BF16)
        y = jnp.dot(lhs, cs_ref[...], preferred_element_type=F32) * scale
        for k1 in range(k1_per_step):
            o_ref[:, k1, gs] = y[k1 * GRID_W:(k1 + 1) * GRID_W]


def _fft_cols(p, f_mat, chan_mat, batch, rows, scale):
    k1_per_step = SUBLANES
    out = pl.pallas_call(
        functools.partial(_fft_cols_kernel, scale, k1_per_step),
        grid=(batch, rows // k1_per_step),
        in_specs=[
            pl.BlockSpec((None, 2, k1_per_step, GRID_W, BRANCH_W), lambda b, j: (b, 0, j, 0, 0)),
            pl.BlockSpec((2 * GRID_W, 2 * GRID_W), lambda b, j: (0, 0)),
            pl.BlockSpec((2 * FNET_GW, FNET_GW), lambda b, j: (0, 0)),
        ],
        out_specs=pl.BlockSpec((None, GRID_W, k1_per_step, BRANCH_W), lambda b, j: (b, 0, j, 0)),
        out_shape=jax.ShapeDtypeStruct((batch, GRID_W, rows, BRANCH_W), F32),
        compiler_params=_params("arbitrary", "arbitrary"),
        name="fft_cols",
    )(p, f_mat, chan_mat)
    return out.reshape(batch * GRID_W * rows, BRANCH_W)


def _fnet_dense_kernel(scale, uv_ref, c_ref, s_ref, o_ref):
    y = jnp.dot(c_ref[...], uv_ref[:, :BRANCH_W].astype(BF16), preferred_element_type=F32)
    y = y - jnp.dot(s_ref[...], uv_ref[:, BRANCH_W:].astype(BF16), preferred_element_type=F32)
    o_ref[...] = y * scale


def _fnet_dense(uv, cos_mat, sin_mat, batch, seq_len, scale):
    return pl.pallas_call(
        functools.partial(_fnet_dense_kernel, scale),
        grid=(batch,),
        in_specs=[pl.BlockSpec((seq_len, 2 * BRANCH_W), lambda b: (b, 0)),
                  pl.BlockSpec((seq_len, seq_len), lambda b: (0, 0)),
                  pl.BlockSpec((seq_len, seq_len), lambda b: (0, 0))],
        out_specs=pl.BlockSpec((seq_len, BRANCH_W), lambda b: (b, 0)),
        out_shape=jax.ShapeDtypeStruct((batch * seq_len, BRANCH_W), F32),
        compiler_params=_params("arbitrary"),
        name="fnet_dense",
    )(uv, cos_mat, sin_mat)


def _dft_cos_sin(n):
    idx = np.arange(n)
    ang = 2.0 * np.pi * ((np.outer(idx, idx)) % n) / n
    return np.cos(ang), np.sin(ang)


def _mxu_const(a):
    return jnp.asarray(a, F32).astype(BF16)


def _use_fft(seq_len):
    return (seq_len // GRID_W) % 16 == 0


def _fourier_mix(parts, u_f32, batch, seq_len):
    scale = 1.0 / math.sqrt(seq_len * FNET_GW)
    rows = seq_len // GRID_W
    c_ch, s_ch = _dft_cos_sin(FNET_GW)
    if _use_fft(seq_len):
        c_r, s_r = _dft_cos_sin(rows)
        eye = np.eye(SUBLANES)
        row_mat = _mxu_const(np.concatenate([np.kron(c_r, eye), np.kron(s_r, eye)], axis=0))
        c_c, s_c = _dft_cos_sin(GRID_W)
        col_mat = _mxu_const(np.block([[c_c, s_c], [-s_c, c_c]]))
        chan_mat = _mxu_const(np.concatenate([c_ch, s_ch], axis=0))
        ang = 2.0 * np.pi * np.outer(np.arange(rows), np.arange(GRID_W)) / seq_len
        ang = ang.reshape(rows, GRID_W // SUBLANES, SUBLANES).transpose(1, 0, 2).reshape(-1, rows * SUBLANES)
        tw_cos = jnp.asarray(np.broadcast_to(np.cos(ang)[:, :, None], ang.shape + (FNET_GW,)), F32)
        tw_sin = jnp.asarray(np.broadcast_to(np.sin(ang)[:, :, None], ang.shape + (FNET_GW,)), F32)
        p = _fft_rows(u_f32, row_mat, tw_cos, tw_sin, batch, rows)
        return _fft_cols(p, col_mat, chan_mat, batch, rows, scale)
    uv = _chan_dft(parts, _mxu_const(np.concatenate([c_ch, s_ch], axis=1)))
    c_l, s_l = _dft_cos_sin(seq_len)
    return _fnet_dense(uv, _mxu_const(c_l), _mxu_const(s_l), batch, seq_len, scale)


def _conv_kernel(tiles_per_seq, tl, scb_ref, scc_ref, scx_ref, cfa_ref, cfb_ref,
                 scc_p, scx_p, cfa_p, cfb_p, scc_n, scx_n, cfa_n, cfb_n,
                 wsc_ref, wcf_ref, ln_ref, osc_ref, ocf_ref, esc_ref, ecf_ref):
    i = pl.program_id(0)
    keep_prev = jnp.where(i % tiles_per_seq == 0, 0.0, 1.0)
    keep_next = jnp.where(i % tiles_per_seq == tiles_per_seq - 1, 0.0, 1.0)
    ext = tl + 2 * HALO_ROWS

    def gated(a_ref, b_ref):
        return a_ref[...].astype(F32) * _sigmoid(b_ref[...].astype(F32))

    def prod(a_ref, b_ref):
        return a_ref[...].astype(F32) * b_ref[...].astype(F32)

    esc_ref[0:HALO_ROWS] = prod(scc_p, scx_p) * keep_prev
    esc_ref[HALO_ROWS:HALO_ROWS + tl] = prod(scc_ref, scx_ref)
    esc_ref[HALO_ROWS + tl:] = prod(scc_n, scx_n) * keep_next
    ecf_ref[0, 0:HALO_ROWS] = gated(cfa_p, cfb_p) * keep_prev
    ecf_ref[0, HALO_ROWS:HALO_ROWS + tl] = gated(cfa_ref, cfb_ref)
    ecf_ref[0, HALO_ROWS + tl:ext] = gated(cfa_n, cfb_n) * keep_next
    for s in range(1, SUBLANES):
        ecf_ref[s, 0:ext - SUBLANES] = ecf_ref[0, s:ext - SUBLANES + s]

    rs = 32
    lane_groups = BRANCH_W // LANES
    for r in range(tl // rs):
        r0 = r * rs
        accs = []
        for lg in range(lane_groups):
            ls = slice(lg * LANES, (lg + 1) * LANES)
            acc = None
            for j in range(SC_WIDTH):
                off = HALO_ROWS + r0 + j - (SC_WIDTH - 1) // 2
                tap = wsc_ref[j:j + 1, ls] * esc_ref[off:off + rs, ls]
                acc = tap if acc is None else acc + tap
            osc_ref[r0:r0 + rs, ls] = (scb_ref[r0:r0 + rs, ls].astype(F32) * acc).astype(osc_ref.dtype)
            acc = None
            for j in range(CF_WIDTH):
                off = HALO_ROWS + r0 + j - (CF_WIDTH - 1) // 2
                s = off % SUBLANES
                tap = wcf_ref[j:j + 1, ls] * ecf_ref[s, off - s:off - s + rs, ls]
                acc = tap if acc is None else acc + tap
            accs.append(acc)
        tot = accs[0]
        for a in accs[1:]:
            tot = tot + a
        mu = jnp.sum(tot, axis=-1, keepdims=True) * (1.0 / BRANCH_W)
        sq = None
        for a in accs:
            d = a - mu
            sq = d * d if sq is None else sq + d * d
        inv = lax.rsqrt(jnp.sum(sq, axis=-1, keepdims=True) * (1.0 / BRANCH_W) + EPS)
        for lg, a in enumerate(accs):
            ls = slice(lg * LANES, (lg + 1) * LANES)
            z = (a - mu) * inv * ln_ref[0:1, ls] + ln_ref[1:2, ls]
            ocf_ref[r0:r0 + rs, ls] = (z * _sigmoid(z)).astype(ocf_ref.dtype)


def _conv_branches(parts, sc_conv, cf_conv, cf_ln, layer, seq_len):
    t = parts.shape[0]
    tl = min(512, seq_len)
    assert seq_len % tl == 0 and tl % HALO_ROWS == 0
    tiles_per_seq = seq_len // tl
    halo_per_tile = tl // HALO_ROWS
    n_halo = t // HALO_ROWS
    main = lambda col: pl.BlockSpec((tl, BRANCH_W), lambda i: (i, col))
    prev = lambda col: pl.BlockSpec((HALO_ROWS, BRANCH_W), lambda i: (jnp.maximum(i * halo_per_tile - 1, 0), col))
    nxt = lambda col: pl.BlockSpec((HALO_ROWS, BRANCH_W),
                                   lambda i: (jnp.minimum((i + 1) * halo_per_tile, n_halo - 1), col))
    per_layer = lambda a: pl.BlockSpec((None,) + a.shape[1:], lambda i: (layer, 0, 0))
    out = pl.BlockSpec((tl, BRANCH_W), lambda i: (i, 0))
    ext = tl + 2 * HALO_ROWS
    return pl.pallas_call(
        functools.partial(_conv_kernel, tiles_per_seq, tl),
        grid=(t // tl,),
        in_specs=[main(5), main(6), main(7), main(8), main(9),
                  prev(6), prev(7), prev(8), prev(9), nxt(6), nxt(7), nxt(8), nxt(9),
                  per_layer(sc_conv), per_layer(cf_conv), per_layer(cf_ln)],
        out_specs=[out, out],
        out_shape=[jax.ShapeDtypeStruct((t, BRANCH_W), BF16)] * 2,
        scratch_shapes=[pltpu.VMEM((ext, BRANCH_W), F32), pltpu.VMEM((SUBLANES, ext, BRANCH_W), F32)],
        compiler_params=_params("arbitrary"),
        name="conv_branches",
    )(*([parts] * 13), sc_conv, cf_conv, cf_ln)


MIX_COL_TILE = 512
MIX_GATE_BLOCK = 1024


def _mix_kernel(cast_blocks, *refs):
    y_refs = refs[:N_BRANCH]
    gates_per_branch = D_MODEL // MIX_GATE_BLOCK
    n_gates = N_BRANCH * gates_per_branch
    n_cast = len(cast_blocks)
    g_refs = refs[N_BRANCH:N_BRANCH + n_gates]
    wb_ref, wo_ref, x_ref, ng_ref, mod_ref = refs[N_BRANCH + n_gates:N_BRANCH + n_gates + 5]
    rest = refs[N_BRANCH + n_gates + 5:]
    cast_src, (xo_ref, h_ref), cast_dst, m_ref = rest[:n_cast], rest[n_cast:n_cast + 2], rest[n_cast + 2:-1], rest[-1]
    _cast_step(pl.program_id(0), cast_blocks, cast_src, cast_dst)
    ys = [y_ref[...].astype(BF16) for y_ref in y_refs]
    for j in range(D_MODEL // MIX_COL_TILE):
        c0 = j * MIX_COL_TILE
        cs = slice(c0, c0 + MIX_COL_TILE)
        merged = None
        for n in range(N_BRANCH):
            p = jnp.dot(ys[n], wb_ref[n, :, cs], preferred_element_type=F32)
            g_ref = g_refs[n * gates_per_branch + c0 // MIX_GATE_BLOCK]
            g0 = c0 % MIX_GATE_BLOCK
            term = g_ref[:, g0:g0 + MIX_COL_TILE].astype(F32) * p
            merged = term if merged is None else merged + term
        m_ref[:, cs] = merged.astype(BF16)
    mix = jnp.dot(m_ref[...], wo_ref[...], preferred_element_type=F32)
    x1 = x_ref[...] + mod_ref[2:3, :] * (_rms(mix) * ng_ref[1:2, :])
    xo_ref[...] = x1
    h = _rms(x1) * ng_ref[2:3, :]
    h_ref[...] = (h * (1.0 + mod_ref[4:5, :]) + mod_ref[3:4, :]).astype(h_ref.dtype)


def _mix(ys, parts, w_branch, w_out, x, norm_g, mod, layer, mod_span, cast_mats=()):
    t, d = x.shape
    tm = min(256, mod_span)
    per_seq = mod_span // tm
    gate0 = GATE_COL0 // MIX_GATE_BLOCK
    n_gate_blocks = N_BRANCH * d // MIX_GATE_BLOCK
    resident = pl.Buffered(1)
    y_spec = pl.BlockSpec((tm, BRANCH_W), lambda i: (i, 0))
    gate = lambda k: pl.BlockSpec((tm, MIX_GATE_BLOCK), lambda i: (i, gate0 + k))
    row_tile = pl.BlockSpec((tm, d), lambda i: (i, 0))
    jobs = _cast_jobs(cast_mats, t // tm)
    cast_src, cast_dst, cast_shapes = _cast_specs(jobs, lambda i: i)
    out = pl.pallas_call(
        functools.partial(_mix_kernel, tuple(job[-1] for job in jobs)),
        grid=(t // tm,),
        in_specs=[y_spec] * N_BRANCH + [gate(k) for k in range(n_gate_blocks)] + [
            pl.BlockSpec((None, N_BRANCH, BRANCH_W, d), lambda i: (layer, 0, 0, 0), pipeline_mode=resident),
            pl.BlockSpec((None, d, d), lambda i: (layer, 0, 0), pipeline_mode=resident),
            row_tile,
            pl.BlockSpec((None, 4, d), lambda i: (layer, 0, 0)),
            pl.BlockSpec((None, 6, d), lambda i: (i // per_seq, 0, 0))] + cast_src,
        out_specs=[row_tile, row_tile] + cast_dst,
        out_shape=[jax.ShapeDtypeStruct((t, d), F32), jax.ShapeDtypeStruct((t, d), BF16)] + cast_shapes,
        scratch_shapes=[pltpu.VMEM((tm, d), BF16)],
        compiler_params=_params("arbitrary"),
        name="mix",
    )(*ys, *([parts] * n_gate_blocks), w_branch, w_out, x, norm_g, mod, *[job[0] for job in jobs])
    return out[0], out[1], out[2:]


def _ffn_kernel(n_f, cast_blocks, h_ref, w1_ref, w2_ref, x_ref, ng_ref, mod_ref, *rest):
    n_cast = len(cast_blocks)
    cast_src, o_ref, cast_dst, acc_ref = rest[:n_cast], rest[n_cast], rest[n_cast + 1:-1], rest[-1]
    f = pl.program_id(1)
    _cast_step(pl.program_id(0) * n_f + f, cast_blocks, cast_src, cast_dst)

    def partial_sum():
        a = jnp.maximum(jnp.dot(h_ref[...], w1_ref[...], preferred_element_type=F32), 0.0)
        return jnp.dot((a * a).astype(BF16), w2_ref[...], preferred_element_type=F32)

    @pl.when(f == 0)
    def _():
        acc_ref[...] = partial_sum()

    @pl.when(f != 0)
    def _():
        acc_ref[...] += partial_sum()

    @pl.when(f == n_f - 1)
    def _():
        o_ref[...] = x_ref[...] + mod_ref[5:6, :] * (_rms(acc_ref[...]) * ng_ref[3:4, :])


def _ffn(h, w1, w2, x, norm_g, mod, layer, mod_span, cast_mats=()):
    t, d = x.shape
    d_ff = w1.shape[1]
    tm = min(512, mod_span)
    tf = 1024
    per_seq = mod_span // tm
    n_i, n_f = t // tm, d_ff // tf
    row_tile = pl.BlockSpec((tm, d), lambda i, f: (i, 0))
    jobs = _cast_jobs(cast_mats, n_i * n_f)
    cast_src, cast_dst, cast_shapes = _cast_specs(jobs, lambda i, f: i * n_f + f)
    out = pl.pallas_call(
        functools.partial(_ffn_kernel, n_f, tuple(job[-1] for job in jobs)),
        grid=(n_i, n_f),
        in_specs=[
            row_tile,
            pl.BlockSpec((d, tf), lambda i, f: (0, f)),
            pl.BlockSpec((tf, d), lambda i, f: (f, 0)),
            row_tile,
            pl.BlockSpec((None, 4, d), lambda i, f: (layer, 0, 0)),
            pl.BlockSpec((None, 6, d), lambda i, f: (i // per_seq, 0, 0)),
        ] + cast_src,
        out_specs=[row_tile] + cast_dst,
        out_shape=[jax.ShapeDtypeStruct((t, d), F32)] + cast_shapes,
        scratch_shapes=[pltpu.VMEM((tm, d), F32)],
        compiler_params=_params("arbitrary", "arbitrary"),
        name="ffn",
    )(h, w1, w2, x, norm_g, mod, *[job[0] for job in jobs])
    return out[0], out[1:]


def _rope_tables(seq_len):
    rows = seq_len // GRID_W
    row = jnp.repeat(jnp.arange(rows, dtype=F32), GRID_W)
    col = jnp.tile(jnp.arange(GRID_W, dtype=F32), rows)
    freqs = ROPE_BASE ** (-jnp.arange(ROPE_PAIRS, dtype=F32) / ROPE_PAIRS)
    ang = jnp.concatenate([row[:, None] * freqs[None], col[:, None] * freqs[None]], axis=-1)
    cos, sin = jnp.cos(ang), jnp.sin(ang)
    return jnp.concatenate([cos, cos], axis=-1), jnp.concatenate([-sin, sin], axis=-1)


def kernel(x, c, ctx, c_ctx, w_ada, b_ada, norm_g, w_in, ret_decay, sc_conv, cf_conv, cf_ln,
           w_branch, w_out, w_ff1, w_ff2):
    batch, seq_len, d = x.shape
    ctx_len = ctx.shape[1]
    depth = w_ada.shape[0]
    in_cols = w_in.shape[2]
    assert d == D_MODEL and seq_len % (GRID_W * 16) == 0 and ctx_len % RET_CHUNK == 0

    n_rows = -(-(batch + 1) // BF16_ROWS) * BF16_ROWS
    c_rows = jnp.concatenate([c, c_ctx[None], jnp.zeros((n_rows - batch - 1, d), F32)], axis=0)
    mod_all = _ada(c_rows, w_ada, b_ada).reshape(depth, n_rows, 6, d)

    rope_tabs = _rope_tables(seq_len)
    xf = x.reshape(batch * seq_len, d)
    xc = ctx.reshape(batch * ctx_len, d)
    zeros_state = jnp.zeros((batch, RET_HEADS, RET_DK, RET_DK), F32)
    wb_b, wo_b = w_branch.astype(BF16), w_out.astype(BF16)
    w_in_l = w_in[0].astype(BF16)
    w1_l = w2_l = None

    def mixer_tail(parts, u_f32, y_ret, x_res, mod, layer, length, mod_span, w1, w2, mix_casts=(), ffn_casts=()):
        y_f = _fourier_mix(parts, u_f32, batch, length)
        y_sc, y_cf = _conv_branches(parts, sc_conv, cf_conv, cf_ln, layer, length)
        x1, h2, mix_cast = _mix((y_ret, y_f, y_sc, y_cf), parts, wb_b, wo_b, x_res, norm_g, mod, layer, mod_span,
                                mix_casts)
        x2, ffn_cast = _ffn(h2, w1, w2, x1, norm_g, mod, layer, mod_span, ffn_casts)
        return x2, mix_cast, ffn_cast

    for layer in range(depth):
        with_ctx = layer < depth - 1
        mod = mod_all[layer, :batch]
        mod_c = mod_all[layer, batch:batch + 1]
        tables = _ret_tables(ret_decay[layer].astype(F32))

        ctx_cols = in_cols if with_ctx else 2 * BRANCH_W
        parts_c, uf_c, _ = _inproj(xc, norm_g, mod_c, w_in_l, layer, ctx_cols, batch * ctx_len,
                                   emit_fnet=with_ctx and _use_fft(ctx_len))
        stc_f, stc_b, s0f, s0b = _ret_state(parts_c, tables, zeros_state, zeros_state, None, batch, ctx_len)

        first_casts = ((w_ff1, 0), (w_ff2, 0)) if layer == 0 else ()
        parts, uf, cast = _inproj(xf, norm_g, mod, w_in_l, layer, in_cols, seq_len, first_casts,
                                  emit_fnet=_use_fft(seq_len))
        if layer == 0:
            w1_l, w2_l = cast
        st_f, st_b, _, _ = _ret_state(parts, tables, s0f, s0b, rope_tabs, batch, seq_len)
        y_ret = _ret_out(parts, tables, st_f, st_b, rope_tabs, batch, seq_len)
        mix_casts = ((w_in, layer + 1),) if with_ctx else ()
        ffn_casts = ((w_ff1, layer + 1), (w_ff2, layer + 1)) if with_ctx else ()
        xf, mix_cast, ffn_cast = mixer_tail(parts, uf, y_ret, xf, mod, layer, seq_len, seq_len, w1_l, w2_l,
                                            mix_casts, ffn_casts)

        if with_ctx:
            y_ret_c = _ret_out(parts_c, tables, stc_f, stc_b, None, batch, ctx_len)
            xc, _, _ = mixer_tail(parts_c, uf_c, y_ret_c, xc, mod_c, layer, ctx_len, batch * ctx_len, w1_l, w2_l)
            (w_in_l,), (w1_l, w2_l) = mix_cast, ffn_cast
    return xf.reshape(batch, seq_len, d)
```
